```python
import math
import jax, jax.numpy as jnp
from jax import lax
import numpy as np

D_MODEL = 1024
BATCH = 8
SEQ = 8192
DEPTH = 4

N_MEM = 256
EPS = 1e-6
SEQ_WIDTH = 2 * D_MODEL
XA_HEADS = 4
XA_HEAD_DIM = D_MODEL // XA_HEADS
XA_WIDTH = XA_HEADS * XA_HEAD_DIM
MIX_WIDTH = SEQ_WIDTH + XA_WIDTH
GLA_HEADS = 4
GLA_DK = (D_MODEL // 2) // GLA_HEADS
GLA_DV = SEQ_WIDTH // GLA_HEADS
GLA_RANK = 16
GLA_TAU = 16.0
GLA_CHUNK = 64
S5_GROUP = 16
S5_GROUPS = SEQ_WIDTH // S5_GROUP
S5_STATE = 64
S5_CHUNK = 128
S5_DT_MIN = 1e-3
S5_DT_MAX = 1e-1
GLA_IN = 2 * GLA_HEADS * GLA_DK + SEQ_WIDTH + GLA_RANK + MIX_WIDTH + XA_WIDTH
S5_IN = SEQ_WIDTH + MIX_WIDTH + XA_WIDTH
N_GLA = (DEPTH + 1) // 2
N_S5 = DEPTH // 2

kernel_name = "hybrid_gla_s5_memxattn_trunk"


def rms_norm(x, w):
    xf = x.astype(jnp.float32)
    y = xf * lax.rsqrt(jnp.mean(xf * xf, axis=-1, keepdims=True) + EPS)
    return (y * w.astype(jnp.float32)).astype(x.dtype)


def split_cols(t, sizes):
    idx = np.cumsum(sizes)[:-1].tolist()
    return jnp.split(t, idx, axis=-1)


def memory_attention(q, mem_n, w_kv):
    b_, l_, _ = q.shape
    k, v = jnp.split(mem_n @ w_kv, 2, axis=-1)
    qh = q.reshape(b_, l_, XA_HEADS, XA_HEAD_DIM)
    kh = k.reshape(b_, -1, XA_HEADS, XA_HEAD_DIM)
    vh = v.reshape(b_, -1, XA_HEADS, XA_HEAD_DIM)
    s = jnp.einsum('blhd,bmhd->bhlm', qh, kh).astype(jnp.float32) * (XA_HEAD_DIM ** -0.5)
    p = jax.nn.softmax(s, axis=-1).astype(v.dtype)
    o = jnp.einsum('bhlm,bmhd->blhd', p, vh)
    return o.reshape(b_, l_, XA_WIDTH)


def gla_chunked(q, k, v, g):
    b_, l_, h_, dk = q.shape
    dv = v.shape[-1]
    nc = l_ // GLA_CHUNK

    def to_chunks(t):
        return t.reshape(b_, nc, GLA_CHUNK, h_, t.shape[-1]).transpose(1, 0, 3, 2, 4)

    qc, kc, vc, gc = to_chunks(q), to_chunks(k), to_chunks(v), to_chunks(g)
    causal = jnp.tril(jnp.ones((GLA_CHUNK, GLA_CHUNK), dtype=bool))

    def step(state, inp):
        qi, ki, vi, gi = inp
        qf, kf, vf = qi.astype(jnp.float32), ki.astype(jnp.float32), vi.astype(jnp.float32)
        bcum = jnp.cumsum(gi.astype(jnp.float32), axis=-2)
        b_last = bcum[..., -1:, :]
        q_dec = qf * jnp.exp(bcum)
        k_inv = kf * jnp.exp(-bcum)
        k_end = kf * jnp.exp(b_last - bcum)
        att = jnp.where(causal, jnp.einsum('bhcd,bhsd->bhcs', q_dec, k_inv), 0.0)
        o = jnp.einsum('bhcs,bhse->bhce', att, vf) + jnp.einsum('bhcd,bhde->bhce', q_dec, state)
        state = jnp.exp(b_last[..., 0, :])[..., None] * state + jnp.einsum('bhcd,bhce->bhde', k_end, vf)
        return state, o.astype(v.dtype)

    s0 = jnp.zeros((b_, h_, dk, dv), jnp.float32)
    _, o = lax.scan(step, s0, (qc, kc, vc, gc))
    return o.transpose(1, 0, 3, 2, 4).reshape(b_, l_, h_, dv)


def gla_branch(h, w_in, w_gate_up, gate_bias, out_norm_w):
    b_, l_, _ = h.shape
    qk = GLA_HEADS * GLA_DK
    q, k, v, r, z, xq = split_cols(h @ w_in, [qk, qk, SEQ_WIDTH, GLA_RANK, MIX_WIDTH, XA_WIDTH])
    g = jax.nn.log_sigmoid((r @ w_gate_up + gate_bias).astype(jnp.float32)) / GLA_TAU
    o = gla_chunked(q.reshape(b_, l_, GLA_HEADS, GLA_DK) * (GLA_DK ** -0.5),
                    k.reshape(b_, l_, GLA_HEADS, GLA_DK),
                    v.reshape(b_, l_, GLA_HEADS, GLA_DV),
                    g.reshape(b_, l_, GLA_HEADS, GLA_DK))
    o = rms_norm(o, out_norm_w).reshape(b_, l_, SEQ_WIDTH)
    return o.astype(h.dtype), z, xq


def s5_discretize(lam_re, lam_im, log_step, b_re, b_im):
    lam_re = lam_re.astype(jnp.float32)
    lam_im = lam_im.astype(jnp.float32)
    dt = jnp.exp(log_step.astype(jnp.float32))[:, None]
    mag = jnp.exp(lam_re * dt)
    ab_re = mag * jnp.cos(lam_im * dt)
    ab_im = mag * jnp.sin(lam_im * dt)
    den = lam_re * lam_re + lam_im * lam_im
    nr, ni = ab_re - 1.0, ab_im
    coef_re = ((nr * lam_re + ni * lam_im) / den)[..., None]
    coef_im = ((ni * lam_re - nr * lam_im) / den)[..., None]
    br, bi = b_re.astype(jnp.float32), b_im.astype(jnp.float32)
    bb_re = coef_re * br - coef_im * bi
    bb_im = coef_re * bi + coef_im * br
    return ab_re, ab_im, bb_re, bb_im


def _linrec_combine(e1, e2):
    a1r, a1i, x1r, x1i = e1
    a2r, a2i, x2r, x2i = e2
    return (a1r * a2r - a1i * a2i,
            a1r * a2i + a1i * a2r,
            a2r * x1r - a2i * x1i + x2r,
            a2r * x1i + a2i * x1r + x2i)


def s5_chunked(u, ab_re, ab_im, bb_re, bb_im, c_re, c_im):
    b_, l_, g_, hg = u.shape
    nc = l_ // S5_CHUNK
    uc = u.reshape(b_, nc, S5_CHUNK, g_, hg).transpose(1, 0, 2, 3, 4)

    def step(carry, u_i):
        cr, ci = carry
        bu_re = jnp.einsum('bcgh,gph->bcgp', u_i, bb_re)
        bu_im = jnp.einsum('bcgh,gph->bcgp', u_i, bb_im)
        a_r = jnp.broadcast_to(ab_re, bu_re.shape)
        a_i = jnp.broadcast_to(ab_im, bu_im.shape)
        acum_r, acum_i, x_r, x_i = lax.associative_scan(_linrec_combine, (a_r, a_i, bu_re, bu_im), axis=1)
        x_r = x_r + acum_r * cr[:, None] - acum_i * ci[:, None]
        x_i = x_i + acum_r * ci[:, None] + acum_i * cr[:, None]
        y = jnp.einsum('bcgp,ghp->bcgh', x_r, c_re) - jnp.einsum('bcgp,ghp->bcgh', x_i, c_im)
        return (x_r[:, -1], x_i[:, -1]), y

    zeros = jnp.zeros((b_, g_, S5_STATE), jnp.float32)
    _, y = lax.scan(step, (zeros, zeros), uc)
    return y.transpose(1, 0, 2, 3, 4).reshape(b_, l_, g_, hg)


def s5_branch(h, w_in, lam_re, lam_im, log_step, b_re, b_im, c_re, c_im, d, w_glu, b_glu):
    b_, l_, _ = h.shape
    u, z, xq = split_cols(h @ w_in, [SEQ_WIDTH, MIX_WIDTH, XA_WIDTH])
    ab_re, ab_im, bb_re, bb_im = s5_discretize(lam_re, lam_im, log_step, b_re, b_im)
    uf = u.astype(jnp.float32)
    y = s5_chunked(uf.reshape(b_, l_, S5_GROUPS, S5_GROUP), ab_re, ab_im, bb_re, bb_im,
                   c_re.astype(jnp.float32), c_im.astype(jnp.float32))
    y = y.reshape(b_, l_, SEQ_WIDTH) + d.astype(jnp.float32) * uf
    y = jax.nn.gelu(y).astype(h.dtype)
    y = y * jax.nn.sigmoid(y @ w_glu + b_glu)
    return y, z, xq


def setup_inputs(seed: int = 0) -> dict:
    key = jax.random.key(seed)
    ks = jax.random.split(key, 24)
    nrm = jax.random.normal
    inv = lambda n: 1.0 / math.sqrt(n)
    n_idx = jnp.arange(S5_STATE, dtype=jnp.float32)
    lam_re = -0.5 * jnp.exp(0.02 * nrm(ks[8], (N_S5, S5_GROUPS, S5_STATE), jnp.float32))
    lam_im = jnp.broadcast_to(math.pi * n_idx, (N_S5, S5_GROUPS, S5_STATE)).astype(jnp.float32)
    log_step = jax.random.uniform(ks[9], (N_S5, S5_GROUPS), jnp.float32,
                                  math.log(S5_DT_MIN), math.log(S5_DT_MAX))
    return {
        "x": nrm(ks[0], (BATCH, SEQ, D_MODEL), jnp.float32),
        "mem": nrm(ks[1], (BATCH, N_MEM, D_MODEL), jnp.float32),
        "norm_w": 1.0 + 0.02 * nrm(ks[2], (DEPTH, D_MODEL), jnp.float32),
        "mem_norm_w": 1.0 + 0.02 * nrm(ks[3], (D_MODEL,), jnp.float32),
        "gla_w_in": nrm(ks[4], (N_GLA, D_MODEL, GLA_IN), jnp.float32) * inv(D_MODEL),
        "gla_w_gate_up": nrm(ks[5], (N_GLA, GLA_RANK, GLA_HEADS * GLA_DK), jnp.float32) * inv(GLA_RANK),
        "gla_gate_bias": 0.1 * nrm(ks[6], (N_GLA, GLA_HEADS * GLA_DK), jnp.float32),
        "gla_out_norm_w": 1.0 + 0.02 * nrm(ks[7], (N_GLA, GLA_DV), jnp.float32),
        "s5_w_in": nrm(ks[10], (N_S5, D_MODEL, S5_IN), jnp.float32) * inv(D_MODEL),
        "s5_lam_re": lam_re,
        "s5_lam_im": lam_im,
        "s5_log_step": log_step,
        "s5_b_re": nrm(ks[11], (N_S5, S5_GROUPS, S5_STATE, S5_GROUP), jnp.float32) * inv(2 * S5_GROUP),
        "s5_b_im": nrm(ks[12], (N_S5, S5_GROUPS, S5_STATE, S5_GROUP), jnp.float32) * inv(2 * S5_GROUP),
        "s5_c_re": 0.5 * nrm(ks[13], (N_S5, S5_GROUPS, S5_GROUP, S5_STATE), jnp.float32),
        "s5_c_im": 0.5 * nrm(ks[14], (N_S5, S5_GROUPS, S5_GROUP, S5_STATE), jnp.float32),
        "s5_d": nrm(ks[15], (N_S5, SEQ_WIDTH), jnp.float32),
        "s5_w_glu": nrm(ks[16], (N_S5, SEQ_WIDTH, SEQ_WIDTH), jnp.float32) * inv(SEQ_WIDTH),
        "s5_b_glu": 0.01 * nrm(ks[17], (N_S5, SEQ_WIDTH), jnp.float32),
        "xa_w_kv": nrm(ks[18], (DEPTH, D_MODEL, 2 * XA_WIDTH), jnp.float32) * inv(D_MODEL),
        "w_out": nrm(ks[19], (DEPTH, MIX_WIDTH, D_MODEL), jnp.float32) * inv(MIX_WIDTH),
        "final_norm_w": 1.0 + 0.02 * nrm(ks[20], (D_MODEL,), jnp.float32),
    }


def reference(x, mem, norm_w, mem_norm_w, gla_w_in, gla_w_gate_up, gla_gate_bias, gla_out_norm_w,
              s5_w_in, s5_lam_re, s5_lam_im, s5_log_step, s5_b_re, s5_b_im, s5_c_re, s5_c_im,
              s5_d, s5_w_glu, s5_b_glu, xa_w_kv, w_out, final_norm_w):
    mem_n = rms_norm(mem, mem_norm_w)
    for i in range(DEPTH):
        h = rms_norm(x, norm_w[i])
        j = i // 2
        if i % 2 == 0:
            seq_out, z, xq = gla_branch(h, gla_w_in[j], gla_w_gate_up[j], gla_gate_bias[j], gla_out_norm_w[j])
        else:
            seq_out, z, xq = s5_branch(h, s5_w_in[j], s5_lam_re[j], s5_lam_im[j], s5_log_step[j],
                                       s5_b_re[j], s5_b_im[j], s5_c_re[j], s5_c_im[j],
                                       s5_d[j], s5_w_glu[j], s5_b_glu[j])
        xa = memory_attention(xq, mem_n, xa_w_kv[i])
        y = jnp.concatenate([seq_out.astype(x.dtype), xa.astype(x.dtype)], axis=-1) * jax.nn.silu(z)
        x = x + y @ w_out[i]
    return rms_norm(x, final_norm_w)
```

```python
import functools
import math

import jax
import jax.numpy as jnp
from jax import lax
from jax.experimental import pallas as pl
from jax.experimental.pallas import tpu as pltpu

F32 = jnp.float32
BF16 = jnp.bfloat16

D_MODEL = 1024
EPS = 1e-6
SEQ_WIDTH = 2 * D_MODEL
XA_HEADS = 4
XA_HEAD_DIM = D_MODEL // XA_HEADS
XA_WIDTH = D_MODEL
MIX_WIDTH = SEQ_WIDTH + XA_WIDTH
GLA_HEADS = 4
GLA_DK = 128
GLA_DV = 512
GLA_QK = GLA_HEADS * GLA_DK
GLA_RANK = 16
GLA_TAU = 16.0
GLA_CHUNK = 64
S5_GROUP = 16
S5_GROUPS = SEQ_WIDTH // S5_GROUP
S5_STATE = 64

LANES = 128
SUBLANES = 8
S5_T = 16
S5_TILE_GROUPS = LANES // S5_GROUP
S5_TILES = SEQ_WIDTH // LANES
S5_K = S5_T * S5_GROUP

PROJ_ROWS = 512
GLA_ROWS = 512
S5_BLOCK = 4096
VMEM_LIMIT = 56 * 1024 * 1024


def _const_spec(shape):
    nd = len(shape)
    return pl.BlockSpec(shape, lambda *_: (0,) * nd, pipeline_mode=pl.Buffered(1))


def _params(n_axes):
    return pltpu.CompilerParams(dimension_semantics=("arbitrary",) * n_axes,
                                vmem_limit_bytes=VMEM_LIMIT)


def _rms_norm_rows(x, w):
    ms = jnp.mean(x * x, axis=-1, keepdims=True)
    return x * lax.rsqrt(ms + EPS) * w


def _mm_store(h, w_ref, lo, hi, out_ref, chunk=512):
    for c in range(lo, hi, chunk):
        out_ref[:, c - lo:c - lo + chunk] = jnp.dot(
            h, w_ref[:, c:c + chunk], preferred_element_type=F32).astype(out_ref.dtype)


def _mem_kv_kernel(mem_ref, nw_ref, w_ref, kv_ref):
    mn = _rms_norm_rows(mem_ref[...], nw_ref[...]).astype(BF16)
    kv_ref[0] = jnp.dot(mn, w_ref[0], preferred_element_type=F32).astype(BF16)


def _mem_kv(mem2, mem_norm_w, w_kv_bf):
    depth = w_kv_bf.shape[0]
    rows, n_mem = mem2.shape[0], 256
    nb = rows // n_mem
    return pl.pallas_call(
        _mem_kv_kernel,
        grid=(depth, nb),
        in_specs=[pl.BlockSpec((n_mem, D_MODEL), lambda i, b: (b, 0)),
                  pl.BlockSpec((1, D_MODEL), lambda i, b: (0, 0)),
                  pl.BlockSpec((1, D_MODEL, 2 * XA_WIDTH), lambda i, b: (i, 0, 0))],
        out_specs=pl.BlockSpec((1, n_mem, 2 * XA_WIDTH), lambda i, b: (i, b, 0)),
        out_shape=jax.ShapeDtypeStruct((depth, rows, 2 * XA_WIDTH), BF16),
        compiler_params=_params(2),
        name="mem_kv",
    )(mem2, mem_norm_w.reshape(1, D_MODEL), w_kv_bf)


def _log_sigmoid(x):
    return jnp.minimum(x, 0.0) - jnp.log(1.0 + jnp.exp(-jnp.abs(x)))


def _gla_proj_kernel(x_ref, nw_ref, w_ref, wr_ref, wg_ref, gb_ref,
                     q_ref, k_ref, v_ref, g_ref, z_ref, xq_ref):
    h = _rms_norm_rows(x_ref[...], nw_ref[...]).astype(BF16)
    _mm_store(h, w_ref, 0, GLA_QK, q_ref)
    _mm_store(h, w_ref, GLA_QK, 2 * GLA_QK, k_ref)
    _mm_store(h, w_ref, 2 * GLA_QK, 2 * GLA_QK + SEQ_WIDTH, v_ref)
    z0 = 2 * GLA_QK + SEQ_WIDTH
    _mm_store(h, w_ref, z0, z0 + MIX_WIDTH, z_ref)
    _mm_store(h, w_ref, z0 + MIX_WIDTH, z0 + MIX_WIDTH + XA_WIDTH, xq_ref)
    r = jnp.dot(h, wr_ref[...], preferred_element_type=F32).astype(BF16)
    pre = jnp.dot(r, wg_ref[...], preferred_element_type=F32) + gb_ref[...]
    g_ref[...] = _log_sigmoid(pre) * (1.0 / GLA_TAU)


def _gla_proj(x2, norm_w, w_in, w_gate_up, gate_bias):
    tok = x2.shape[0]
    tm = PROJ_ROWS
    qk = GLA_QK
    r0 = 2 * qk + SEQ_WIDTH
    w_main = jnp.concatenate([w_in[:, :r0], w_in[:, r0 + GLA_RANK:]], axis=1).astype(BF16)
    w_r = jnp.pad(w_in[:, r0:r0 + GLA_RANK], ((0, 0), (0, LANES - GLA_RANK))).astype(BF16)
    w_g = jnp.pad(w_gate_up, ((0, LANES - GLA_RANK), (0, 0))).astype(BF16)
    n_main = w_main.shape[1]
    row = lambda w: pl.BlockSpec((tm, w), lambda i: (i, 0))
    return pl.pallas_call(
        _gla_proj_kernel,
        grid=(tok // tm,),
        in_specs=[row(D_MODEL), _const_spec((1, D_MODEL)), _const_spec((D_MODEL, n_main)),
                  _const_spec((D_MODEL, LANES)), _const_spec((LANES, qk)), _const_spec((1, qk))],
        out_specs=[row(qk), row(qk), row(SEQ_WIDTH), row(qk), row(MIX_WIDTH), row(XA_WIDTH)],
        out_shape=[jax.ShapeDtypeStruct((tok, qk), BF16), jax.ShapeDtypeStruct((tok, qk), BF16),
                   jax.ShapeDtypeStruct((tok, SEQ_WIDTH), BF16), jax.ShapeDtypeStruct((tok, qk), F32),
                   jax.ShapeDtypeStruct((tok, MIX_WIDTH), BF16), jax.ShapeDtypeStruct((tok, XA_WIDTH), BF16)],
        compiler_params=_params(1),
        name="gla_proj",
    )(x2, norm_w.reshape(1, D_MODEL), w_main, w_r, w_g, gate_bias.reshape(1, qk))


def _dot_nt(a, b):
    return lax.dot_general(a, b, (((1,), (1,)), ((), ())), preferred_element_type=F32)


def _dot_tn(a, b):
    return lax.dot_general(a, b, (((0,), (0,)), ((), ())), preferred_element_type=F32)


def _gla_kernel(q_ref, k_ref, v_ref, g_ref, onw_ref, o_ref, st_ref, *, rows):
    @pl.when(pl.program_id(1) == 0)
    def _():
        st_ref[...] = jnp.zeros_like(st_ref)

    c = GLA_CHUNK
    ri = lax.broadcasted_iota(jnp.int32, (c, c), 0)
    ci = lax.broadcasted_iota(jnp.int32, (c, c), 1)
    causal = ri >= ci
    tri = causal.astype(BF16)
    scale = GLA_DK ** -0.5
    onw = onw_ref[...]

    def chunk(ic, carry):
        r0 = pl.multiple_of(ic * c, c)
        gc = g_ref[pl.ds(r0, c), :]
        g_hi = gc.astype(BF16)
        g_lo = (gc - g_hi.astype(F32)).astype(BF16)
        bcum = (jnp.dot(tri, g_hi, preferred_element_type=F32)
                + jnp.dot(tri, g_lo, preferred_element_type=F32))
        b_last = bcum[c - 1:c, :]
        qc = q_ref[pl.ds(r0, c), :].astype(F32)
        kc = k_ref[pl.ds(r0, c), :].astype(F32)
        q_dec = (qc * (jnp.exp(bcum) * scale)).astype(BF16)
        k_inv = (kc * jnp.exp(-bcum)).astype(BF16)
        k_end = (kc * jnp.exp(b_last - bcum)).astype(BF16)
        e_last = jnp.exp(b_last)
        for h in range(GLA_HEADS):
            ks = slice(h * GLA_DK, (h + 1) * GLA_DK)
            vs = slice(h * GLA_DV, (h + 1) * GLA_DV)
            att = jnp.where(causal, _dot_nt(q_dec[:, ks], k_inv[:, ks]), 0.0).astype(BF16)
            vh = v_ref[pl.ds(r0, c), vs]
            st = st_ref[h]
            o = (jnp.dot(att, vh, preferred_element_type=F32)
                 + _dot_nt(q_dec[:, ks], st.astype(BF16)))
            st_ref[h] = st * e_last[:, ks] + _dot_tn(vh, k_end[:, ks])
            ms = jnp.mean(o * o, axis=-1, keepdims=True)
            o_ref[pl.ds(r0, c), vs] = (o * lax.rsqrt(ms + EPS) * onw).astype(o_ref.dtype)
        return carry

    lax.fori_loop(0, rows // c, chunk, 0)


def _gla(q, k, v, g, out_norm_w, batch, seq):
    rows = min(GLA_ROWS, seq)
    nblk = seq // rows
    row = lambda w: pl.BlockSpec((rows, w), lambda b, i: (b * nblk + i, 0))
    return pl.pallas_call(
        functools.partial(_gla_kernel, rows=rows),
        grid=(batch, nblk),
        in_specs=[row(GLA_QK), row(GLA_QK), row(SEQ_WIDTH), row(GLA_QK), _const_spec((1, GLA_DV))],
        out_specs=row(SEQ_WIDTH),
        out_shape=jax.ShapeDtypeStruct((batch * seq, SEQ_WIDTH), BF16),
        scratch_shapes=[pltpu.VMEM((GLA_HEADS, GLA_DV, GLA_DK), F32)],
        compiler_params=_params(2),
        name="gla_scan",
    )(q, k, v, g, out_norm_w.reshape(1, GLA_DV))


def _s5_proj_kernel(x_ref, nw_ref, w_ref, u_ref, z_ref, xq_ref):
    h = _rms_norm_rows(x_ref[...], nw_ref[...]).astype(BF16)
    for j in range(0, S5_TILES, 2):
        uu = jnp.dot(h, w_ref[:, j * LANES:(j + 2) * LANES], preferred_element_type=F32)
        u_ref[j] = uu[:, :LANES]
        u_ref[j + 1] = uu[:, LANES:]
    _mm_store(h, w_ref, SEQ_WIDTH, SEQ_WIDTH + MIX_WIDTH, z_ref)
    _mm_store(h, w_ref, SEQ_WIDTH + MIX_WIDTH, SEQ_WIDTH + MIX_WIDTH + XA_WIDTH, xq_ref)


def _s5_proj(x2, norm_w, w_in):
    tok = x2.shape[0]
    tm = PROJ_ROWS
    n_in = w_in.shape[1]
    row = lambda w: pl.BlockSpec((tm, w), lambda i: (i, 0))
    return pl.pallas_call(
        _s5_proj_kernel,
        grid=(tok // tm,),
        in_specs=[row(D_MODEL), _const_spec((1, D_MODEL)), _const_spec((D_MODEL, n_in))],
        out_specs=[pl.BlockSpec((S5_TILES, tm, LANES), lambda i: (0, i, 0)), row(MIX_WIDTH), row(XA_WIDTH)],
        out_shape=[jax.ShapeDtypeStruct((S5_TILES, tok, LANES), F32),
                   jax.ShapeDtypeStruct((tok, MIX_WIDTH), BF16), jax.ShapeDtypeStruct((tok, XA_WIDTH), BF16)],
        compiler_params=_params(1),
        name="s5_proj",
    )(x2, norm_w.reshape(1, D_MODEL), w_in.astype(BF16))


def _s5_operators(lam_re, lam_im, log_step, b_re, b_im, c_re, c_im):
    hp = lax.Precision.HIGHEST
    t = S5_T
    dt = jnp.exp(log_step)[:, None]

    def ab_pow(m):
        m = jnp.asarray(m, F32)[..., None, None]
        mag = jnp.exp(lam_re * dt * m)
        ang = lam_im * dt * m
        return mag * jnp.cos(ang), mag * jnp.sin(ang)

    a_re, a_im = ab_pow(1.0)
    den = lam_re * lam_re + lam_im * lam_im
    nr, ni = a_re - 1.0, a_im
    coef_re = ((nr * lam_re + ni * lam_im) / den)[..., None]
    coef_im = ((ni * lam_re - nr * lam_im) / den)[..., None]
    bb_re = coef_re * b_re - coef_im * b_im
    bb_im = coef_re * b_im + coef_im * b_re

    pr, pi_ = ab_pow(jnp.arange(t + 1))
    pr = jnp.transpose(pr, (1, 0, 2))[:, :, None, :]
    pi_ = jnp.transpose(pi_, (1, 0, 2))[:, :, None, :]
    ca_re = c_re[:, None] * pr - c_im[:, None] * pi_
    ca_im = c_re[:, None] * pi_ + c_im[:, None] * pr
    read = jnp.concatenate([ca_re, -ca_im], axis=-1)
    bb = jnp.concatenate([bb_re, bb_im], axis=1)
    kmat = jnp.einsum('gmhp,gpk->gmhk', read[:, :t], bb, precision=hp)
    lag = jnp.arange(t)[:, None] - jnp.arange(t)[None, :]
    blocks = kmat[:, jnp.clip(lag, 0, t - 1)]
    blocks = jnp.where((lag >= 0)[None, :, :, None, None], blocks, 0.0)
    conv = jnp.transpose(blocks, (0, 1, 3, 2, 4)).reshape(S5_GROUPS, S5_K, S5_K)

    wr, wi = ab_pow(t - 1 - jnp.arange(t))
    wr = jnp.transpose(wr, (1, 2, 0))[..., None]
    wi = jnp.transpose(wi, (1, 2, 0))[..., None]
    st_re = wr * bb_re[:, :, None, :] - wi * bb_im[:, :, None, :]
    st_im = wr * bb_im[:, :, None, :] + wi * bb_re[:, :, None, :]
    to_st = jnp.concatenate([st_re, st_im], axis=1).reshape(S5_GROUPS, 2 * S5_STATE, S5_K)

    from_st = read[:, 1:].reshape(S5_GROUPS, S5_K, 2 * S5_STATE)

    qr, qi = ab_pow(t * jnp.arange(1, SUBLANES + 1))
    tiles = lambda a: jnp.transpose(a, (1, 0, 2)).reshape(
        S5_TILES, S5_TILE_GROUPS, SUBLANES, S5_STATE).transpose(0, 2, 1, 3).reshape(
        S5_TILES, SUBLANES, S5_TILE_GROUPS * S5_STATE)
    tile4 = lambda a: a.reshape((S5_TILES, S5_TILE_GROUPS) + a.shape[1:])
    return (tile4(conv.astype(BF16)), tile4(to_st.astype(BF16)), tile4(from_st.astype(BF16)),
            tiles(qr), tiles(qi))


def _s5_core_kernel(u_ref, conv_ref, tost_ref, fromst_ref, pr_ref, pi_ref, y_ref,
                    cr_ref, ci_ref, zr_ref, zi_ref, sr_ref, si_ref, r_ref, *, n):
    @pl.when(pl.program_id(2) == 0)
    def _():
        cr_ref[...] = jnp.zeros_like(cr_ref)
        ci_ref[...] = jnp.zeros_like(ci_ref)

    half = S5_STATE
    ng = S5_TILE_GROUPS
    at = [u_ref[0, pl.ds(s, n, stride=S5_T), :].T for s in range(S5_T)]
    z = []
    for g in range(ng):
        rg = jnp.concatenate([a[g * S5_GROUP:(g + 1) * S5_GROUP, :] for a in at], axis=0).astype(BF16)
        r_ref[g] = rg
        z.append(jnp.dot(tost_ref[0, g], rg, preferred_element_type=F32))
    for q in range(ng // 2):
        ls = slice(q * LANES, (q + 1) * LANES)
        zr_ref[:, ls] = jnp.concatenate([z[2 * q][:half], z[2 * q + 1][:half]], axis=0).T
        zi_ref[:, ls] = jnp.concatenate([z[2 * q][half:], z[2 * q + 1][half:]], axis=0).T

    pr = pr_ref[0]
    pi_ = pi_ref[0]
    width = ng * half
    rows = lax.broadcasted_iota(jnp.int32, (SUBLANES, width), 0)

    def cmul_add(xr, xi, ar, ai, br, bi):
        return xr + ar * br - ai * bi, xi + ar * bi + ai * br

    def tile_step(it, carry):
        c_r, c_i = carry
        r0 = pl.multiple_of(it * SUBLANES, SUBLANES)
        xr = zr_ref[pl.ds(r0, SUBLANES), :]
        xi = zi_ref[pl.ds(r0, SUBLANES), :]
        for k in (1, 2, 4):
            keep = rows >= k
            shr = jnp.where(keep, pltpu.roll(xr, k, 0), 0.0)
            shi = jnp.where(keep, pltpu.roll(xi, k, 0), 0.0)
            xr, xi = cmul_add(xr, xi, pr[k - 1:k, :], pi_[k - 1:k, :], shr, shi)
        s_r, s_i = cmul_add(xr, xi, pr, pi_, c_r, c_i)
        first = rows == 0
        sr_ref[pl.ds(r0, SUBLANES), :] = jnp.where(first, c_r, pltpu.roll(s_r, 1, 0))
        si_ref[pl.ds(r0, SUBLANES), :] = jnp.where(first, c_i, pltpu.roll(s_i, 1, 0))
        return (jnp.broadcast_to(s_r[SUBLANES - 1:, :], (SUBLANES, width)),
                jnp.broadcast_to(s_i[SUBLANES - 1:, :], (SUBLANES, width)))

    c_r, c_i = lax.fori_loop(0, n // SUBLANES, tile_step, (cr_ref[...], ci_ref[...]))
    cr_ref[...] = c_r
    ci_ref[...] = c_i

    yt = []
    for q in range(ng // 2):
        ls = slice(q * LANES, (q + 1) * LANES)
        srt = sr_ref[:, ls].T
        sit = si_ref[:, ls].T
        for e in range(2):
            g = 2 * q + e
            hs = slice(e * half, (e + 1) * half)
            sprev = jnp.concatenate([srt[hs], sit[hs]], axis=0).astype(BF16)
            yt.append(jnp.dot(conv_ref[0, g], r_ref[g], preferred_element_type=F32)
                      + jnp.dot(fromst_ref[0, g], sprev, preferred_element_type=F32))
    for t in range(S5_T):
        blk = jnp.concatenate([y[t * S5_GROUP:(t + 1) * S5_GROUP, :] for y in yt], axis=0)
        y_ref[0, pl.ds(t, n, stride=S5_T), :] = blk.T


def _s5_core(u3, ops, batch, seq):
    conv, to_st, from_st, pow_re, pow_im = ops
    blk = min(S5_BLOCK, seq)
    nblk = seq // blk
    n = blk // S5_T
    width = S5_TILE_GROUPS * S5_STATE
    tok_spec = pl.BlockSpec((1, blk, LANES), lambda j, b, i: (j, b * nblk + i, 0))
    tile_spec = lambda s: pl.BlockSpec((1,) + s, lambda j, b, i: (j,) + (0,) * len(s))
    return pl.pallas_call(
        functools.partial(_s5_core_kernel, n=n),
        grid=(S5_TILES, batch, nblk),
        in_specs=[tok_spec, tile_spec((S5_TILE_GROUPS, S5_K, S5_K)),
                  tile_spec((S5_TILE_GROUPS, 2 * S5_STATE, S5_K)),
                  tile_spec((S5_TILE_GROUPS, S5_K, 2 * S5_STATE)),
                  tile_spec((SUBLANES, width)), tile_spec((SUBLANES, width))],
        out_specs=tok_spec,
        out_shape=jax.ShapeDtypeStruct(u3.shape, F32),
        scratch_shapes=[pltpu.VMEM((SUBLANES, width), F32), pltpu.VMEM((SUBLANES, width), F32),
                        pltpu.VMEM((n, width), F32), pltpu.VMEM((n, width), F32),
                        pltpu.VMEM((n, width), F32), pltpu.VMEM((n, width), F32),
                        pltpu.VMEM((S5_TILE_GROUPS, S5_K, n), BF16)],
        compiler_params=_params(3),
        name="s5_core",
    )(u3, conv, to_st, from_st, pow_re, pow_im)


def _gelu_tanh(x):
    return 0.5 * x * (1.0 + jnp.tanh(math.sqrt(2.0 / math.pi) * (x + 0.044715 * (x * x * x))))


def _s5_post_kernel(y_ref, u_ref, d_ref, wg_ref, bg_ref, o_ref, act_ref):
    for j in range(S5_TILES):
        ls = slice(j * LANES, (j + 1) * LANES)
        act_ref[:, ls] = _gelu_tanh(y_ref[j] + d_ref[:, ls] * u_ref[j])
    act = act_ref[...]
    gate = jax.nn.sigmoid(jnp.dot(act.astype(BF16), wg_ref[...], preferred_element_type=F32) + bg_ref[...])
    o_ref[...] = (act * gate).astype(o_ref.dtype)


def _s5_post(y3, u3, d, w_glu, b_glu):
    tok = y3.shape[1]
    tm = PROJ_ROWS
    t3 = pl.BlockSpec((S5_TILES, tm, LANES), lambda i: (0, i, 0))
    return pl.pallas_call(
        _s5_post_kernel,
        grid=(tok // tm,),
        in_specs=[t3, t3, _const_spec((1, SEQ_WIDTH)), _const_spec((SEQ_WIDTH, SEQ_WIDTH)),
                  _const_spec((1, SEQ_WIDTH))],
        out_specs=pl.BlockSpec((tm, SEQ_WIDTH), lambda i: (i, 0)),
        out_shape=jax.ShapeDtypeStruct((tok, SEQ_WIDTH), BF16),
        scratch_shapes=[pltpu.VMEM((tm, SEQ_WIDTH), F32)],
        compiler_params=_params(1),
        name="s5_post",
    )(y3, u3, d.reshape(1, SEQ_WIDTH), w_glu.astype(BF16), b_glu.reshape(1, SEQ_WIDTH))


def _mix_kernel(seq_ref, z_ref, xq_ref, k_ref, v_ref, wo_ref, x_ref, fw_ref, o_ref, *, final):
    z = z_ref[...].astype(F32)
    gate = z * jax.nn.sigmoid(z)
    y_seq = (seq_ref[...].astype(F32) * gate[:, :SEQ_WIDTH]).astype(BF16)
    acc = x_ref[...] + jnp.dot(y_seq, wo_ref[:SEQ_WIDTH, :], preferred_element_type=F32)
    for h in range(XA_HEADS):
        hs = slice(h * XA_HEAD_DIM, (h + 1) * XA_HEAD_DIM)
        s = _dot_nt(xq_ref[:, hs], k_ref[0, :, hs]) * (XA_HEAD_DIM ** -0.5)
        s = s - jnp.max(s, axis=-1, keepdims=True)
        p = jnp.exp(s)
        p = (p / jnp.sum(p, axis=-1, keepdims=True)).astype(BF16)
        xa = jnp.dot(p, v_ref[0, :, hs], preferred_element_type=F32)
        y_xa = (xa * gate[:, SEQ_WIDTH + h * XA_HEAD_DIM:SEQ_WIDTH + (h + 1) * XA_HEAD_DIM]).astype(BF16)
        acc = acc + jnp.dot(y_xa, wo_ref[SEQ_WIDTH + h * XA_HEAD_DIM:SEQ_WIDTH + (h + 1) * XA_HEAD_DIM, :],
                            preferred_element_type=F32)
    if final:
        acc = _rms_norm_rows(acc, fw_ref[...])
    o_ref[...] = acc


def _mix(seq_out, z, xq, kv, layer, w_out, x2, final_w, batch, seq, final):
    tok = x2.shape[0]
    tm = min(PROJ_ROWS, seq)
    per_b = seq // tm
    row = lambda w: pl.BlockSpec((tm, w), lambda i: (i, 0))
    n_mem = kv.shape[1] // batch
    k_spec = pl.BlockSpec((1, n_mem, XA_WIDTH), lambda i: (layer, i // per_b, 0))
    v_spec = pl.BlockSpec((1, n_mem, XA_WIDTH), lambda i: (layer, i // per_b, 1))
    return pl.pallas_call(
        functools.partial(_mix_kernel, final=final),
        grid=(tok // tm,),
        in_specs=[row(SEQ_WIDTH), row(MIX_WIDTH), row(XA_WIDTH), k_spec, v_spec,
                  _const_spec((MIX_WIDTH, D_MODEL)), row(D_MODEL), _const_spec((1, D_MODEL))],
        out_specs=row(D_MODEL),
        out_shape=jax.ShapeDtypeStruct((tok, D_MODEL), F32),
        compiler_params=_params(1),
        name="mix_final" if final else "mix",
    )(seq_out, z, xq, kv, kv, w_out.astype(BF16), x2, final_w.reshape(1, D_MODEL))


def kernel(x, mem, norm_w, mem_norm_w, gla_w_in, gla_w_gate_up, gla_gate_bias, gla_out_norm_w, s5_w_in, s5_lam_re, s5_lam_im, s5_log_step, s5_b_re, s5_b_im, s5_c_re, s5_c_im, s5_d, s5_w_glu, s5_b_glu, xa_w_kv, w_out, final_norm_w):
    batch, seq, _ = x.shape
    depth = norm_w.shape[0]
    x2 = x.reshape(batch * seq, D_MODEL)
    kv = _mem_kv(mem.reshape(-1, D_MODEL), mem_norm_w, xa_w_kv.astype(BF16))
    for i in range(depth):
        j = i // 2
        if i % 2 == 0:
            q, k, v, g, z, xq = _gla_proj(x2, norm_w[i], gla_w_in[j], gla_w_gate_up[j], gla_gate_bias[j])
            seq_out = _gla(q, k, v, g, gla_out_norm_w[j], batch, seq)
        else:
            u3, z, xq = _s5_proj(x2, norm_w[i], s5_w_in[j])
            ops = _s5_operators(s5_lam_re[j], s5_lam_im[j], s5_log_step[j], s5_b_re[j], s5_b_im[j],
                                s5_c_re[j], s5_c_im[j])
            y3 = _s5_core(u3, ops, batch, seq)
            seq_out = _s5_post(y3, u3, s5_d[j], s5_w_glu[j], s5_b_glu[j])
        x2 = _mix(seq_out, z, xq, kv, i, w_out[i], x2, final_norm_w, batch, seq, final=(i == depth - 1))
    return x2.reshape(batch, seq, D_MODEL)
```

```python
import functools
import math

import jax
import jax.numpy as jnp
from jax import lax
from jax.experimental import pallas as pl
from jax.experimental.pallas import tpu as pltpu

F32 = jnp.float32
BF16 = jnp.bfloat16

D_MODEL = 1024
EPS = 1e-6
SEQ_WIDTH = 2 * D_MODEL
XA_HEADS = 4
XA_HEAD_DIM = D_MODEL // XA_HEADS
XA_WIDTH = D_MODEL
MIX_WIDTH = SEQ_WIDTH + XA_WIDTH
GLA_HEADS = 4
GLA_DK = 128
GLA_DV = 512
GLA_QK = GLA_HEADS * GLA_DK
GLA_RANK = 16
GLA_TAU = 16.0
GLA_CHUNK = 64
S5_GROUP = 16
S5_GROUPS = SEQ_WIDTH // S5_GROUP
S5_STATE = 64

LANES = 128
SUBLANES = 8
S5_T = 16
S5_TILE_GROUPS = LANES // S5_GROUP
S5_TILES = SEQ_WIDTH // LANES
S5_K = S5_T * S5_GROUP

PROJ_ROWS = 512
GLA_ROWS = 512
S5_BLOCK = 4096
VMEM_LIMIT = 56 * 1024 * 1024


def _const_spec(shape):
    nd = len(shape)
    return pl.BlockSpec(shape, lambda *_: (0,) * nd, pipeline_mode=pl.Buffered(1))


def _params(n_axes):
    return pltpu.CompilerParams(dimension_semantics=("arbitrary",) * n_axes,
                                vmem_limit_bytes=VMEM_LIMIT)


def _rms_norm_rows(x, w):
    ms = jnp.mean(x * x, axis=-1, keepdims=True)
    return x * lax.rsqrt(ms + EPS) * w


def _mm_store(h, w_ref, lo, hi, out_ref, chunk=512):
    for c in range(lo, hi, chunk):
        out_ref[:, c - lo:c - lo + chunk] = jnp.dot(
            h, w_ref[:, c:c + chunk], preferred_element_type=F32).astype(out_ref.dtype)


def _mem_kv_kernel(mem_ref, nw_ref, w_ref, kv_ref):
    mn = _rms_norm_rows(mem_ref[...], nw_ref[...]).astype(BF16)
    kv_ref[0] = jnp.dot(mn, w_ref[0], preferred_element_type=F32).astype(BF16)


def _mem_kv(mem2, mem_norm_w, w_kv_bf):
    depth = w_kv_bf.shape[0]
    rows, n_mem = mem2.shape[0], 256
    nb = rows // n_mem
    return pl.pallas_call(
        _mem_kv_kernel,
        grid=(depth, nb),
        in_specs=[pl.BlockSpec((n_mem, D_MODEL), lambda i, b: (b, 0)),
                  pl.BlockSpec((1, D_MODEL), lambda i, b: (0, 0)),
                  pl.BlockSpec((1, D_MODEL, 2 * XA_WIDTH), lambda i, b: (i, 0, 0))],
        out_specs=pl.BlockSpec((1, n_mem, 2 * XA_WIDTH), lambda i, b: (i, b, 0)),
        out_shape=jax.ShapeDtypeStruct((depth, rows, 2 * XA_WIDTH), BF16),
        compiler_params=_params(2),
        name="mem_kv",
    )(mem2, mem_norm_w.reshape(1, D_MODEL), w_kv_bf)


def _log_sigmoid(x):
    return jnp.minimum(x, 0.0) - jnp.log(1.0 + jnp.exp(-jnp.abs(x)))


def _gla_proj_kernel(x_ref, nw_ref, w_ref, wr_ref, wg_ref, gb_ref,
                     q_ref, k_ref, v_ref, g_ref, z_ref, xq_ref):
    h = _rms_norm_rows(x_ref[...], nw_ref[...]).astype(BF16)
    _mm_store(h, w_ref, 0, GLA_QK, q_ref)
    _mm_store(h, w_ref, GLA_QK, 2 * GLA_QK, k_ref)
    _mm_store(h, w_ref, 2 * GLA_QK, 2 * GLA_QK + SEQ_WIDTH, v_ref)
    z0 = 2 * GLA_QK + SEQ_WIDTH
    _mm_store(h, w_ref, z0, z0 + MIX_WIDTH, z_ref)
    _mm_store(h, w_ref, z0 + MIX_WIDTH, z0 + MIX_WIDTH + XA_WIDTH, xq_ref)
    r = jnp.dot(h, wr_ref[...], preferred_element_type=F32).astype(BF16)
    pre = jnp.dot(r, wg_ref[...], preferred_element_type=F32) + gb_ref[...]
    g_ref[...] = _log_sigmoid(pre) * (1.0 / GLA_TAU)


def _gla_proj(x2, norm_w, w_in, w_gate_up, gate_bias):
    tok = x2.shape[0]
    tm = PROJ_ROWS
    qk = GLA_QK
    r0 = 2 * qk + SEQ_WIDTH
    w_main = jnp.concatenate([w_in[:, :r0], w_in[:, r0 + GLA_RANK:]], axis=1).astype(BF16)
    w_r = jnp.pad(w_in[:, r0:r0 + GLA_RANK], ((0, 0), (0, LANES - GLA_RANK))).astype(BF16)
    w_g = jnp.pad(w_gate_up, ((0, LANES - GLA_RANK), (0, 0))).astype(BF16)
    n_main = w_main.shape[1]
    row = lambda w: pl.BlockSpec((tm, w), lambda i: (i, 0))
    return pl.pallas_call(
        _gla_proj_kernel,
        grid=(tok // tm,),
        in_specs=[row(D_MODEL), _const_spec((1, D_MODEL)), _const_spec((D_MODEL, n_main)),
                  _const_spec((D_MODEL, LANES)), _const_spec((LANES, qk)), _const_spec((1, qk))],
        out_specs=[row(qk), row(qk), row(SEQ_WIDTH), row(qk), row(MIX_WIDTH), row(XA_WIDTH)],
        out_shape=[jax.ShapeDtypeStruct((tok, qk), BF16), jax.ShapeDtypeStruct((tok, qk), BF16),
                   jax.ShapeDtypeStruct((tok, SEQ_WIDTH), BF16), jax.ShapeDtypeStruct((tok, qk), F32),
                   jax.ShapeDtypeStruct((tok, MIX_WIDTH), BF16), jax.ShapeDtypeStruct((tok, XA_WIDTH), BF16)],
        compiler_params=_params(1),
        name="gla_proj",
    )(x2, norm_w.reshape(1, D_MODEL), w_main, w_r, w_g, gate_bias.reshape(1, qk))


def _dot_nt(a, b):
    return lax.dot_general(a, b, (((1,), (1,)), ((), ())), preferred_element_type=F32)


def _dot_tn(a, b):
    return lax.dot_general(a, b, (((0,), (0,)), ((), ())), preferred_element_type=F32)


def _gla_kernel(q_ref, k_ref, v_ref, g_ref, onw_ref, o_ref, st_ref, qd_ref, el_ref, att_ref, kv_ref, *, rows):
    @pl.when(pl.program_id(1) == 0)
    def _():
        st_ref[...] = jnp.zeros_like(st_ref)

    c = GLA_CHUNK
    ri = lax.broadcasted_iota(jnp.int32, (c, c), 0)
    ci = lax.broadcasted_iota(jnp.int32, (c, c), 1)
    causal = ri >= ci
    tri = causal.astype(BF16)
    scale = GLA_DK ** -0.5
    onw = onw_ref[...]
    n_chunks = rows // c

    for ic in range(n_chunks):
        rs = slice(ic * c, (ic + 1) * c)
        gc = g_ref[rs, :]
        g_hi = gc.astype(BF16)
        g_lo = (gc - g_hi.astype(F32)).astype(BF16)
        bcum = (jnp.dot(tri, g_hi, preferred_element_type=F32)
                + jnp.dot(tri, g_lo, preferred_element_type=F32))
        b_last = bcum[c - 1:c, :]
        qc = q_ref[rs, :].astype(F32)
        kc = k_ref[rs, :].astype(F32)
        q_dec = (qc * (jnp.exp(bcum) * scale)).astype(BF16)
        k_inv = (kc * jnp.exp(-bcum)).astype(BF16)
        k_end = (kc * jnp.exp(b_last - bcum)).astype(BF16)
        qd_ref[rs, :] = q_dec
        el_ref[ic:ic + 1, :] = jnp.exp(b_last)
        for h in range(GLA_HEADS):
            ks = slice(h * GLA_DK, (h + 1) * GLA_DK)
            vs = slice(h * GLA_DV, (h + 1) * GLA_DV)
            att_ref[ic, h] = jnp.where(causal, _dot_nt(q_dec[:, ks], k_inv[:, ks]), 0.0).astype(BF16)
            kv_ref[ic, h] = _dot_tn(v_ref[rs, vs], k_end[:, ks])

    for ic in range(n_chunks):
        rs = slice(ic * c, (ic + 1) * c)
        e_last = el_ref[ic:ic + 1, :]
        for h in range(GLA_HEADS):
            ks = slice(h * GLA_DK, (h + 1) * GLA_DK)
            vs = slice(h * GLA_DV, (h + 1) * GLA_DV)
            st = st_ref[h]
            o = (jnp.dot(att_ref[ic, h], v_ref[rs, vs], preferred_element_type=F32)
                 + _dot_nt(qd_ref[rs, ks], st.astype(BF16)))
            st_ref[h] = st * e_last[:, ks] + kv_ref[ic, h]
            ms = jnp.mean(o * o, axis=-1, keepdims=True)
            o_ref[rs, vs] = (o * lax.rsqrt(ms + EPS) * onw).astype(o_ref.dtype)


def _gla(q, k, v, g, out_norm_w, batch, seq):
    rows = min(GLA_ROWS, seq)
    nblk = seq // rows
    n_chunks = rows // GLA_CHUNK
    row = lambda w: pl.BlockSpec((rows, w), lambda b, i: (b * nblk + i, 0))
    return pl.pallas_call(
        functools.partial(_gla_kernel, rows=rows),
        grid=(batch, nblk),
        in_specs=[row(GLA_QK), row(GLA_QK), row(SEQ_WIDTH), row(GLA_QK), _const_spec((1, GLA_DV))],
        out_specs=row(SEQ_WIDTH),
        out_shape=jax.ShapeDtypeStruct((batch * seq, SEQ_WIDTH), BF16),
        scratch_shapes=[pltpu.VMEM((GLA_HEADS, GLA_DV, GLA_DK), F32),
                        pltpu.VMEM((rows, GLA_QK), BF16),
                        pltpu.VMEM((n_chunks, GLA_QK), F32),
                        pltpu.VMEM((n_chunks, GLA_HEADS, GLA_CHUNK, GLA_CHUNK), BF16),
                        pltpu.VMEM((n_chunks, GLA_HEADS, GLA_DV, GLA_DK), F32)],
        compiler_params=_params(2),
        name="gla_scan",
    )(q, k, v, g, out_norm_w.reshape(1, GLA_DV))


def _s5_proj_kernel(x_ref, nw_ref, w_ref, u_ref, z_ref, xq_ref):
    h = _rms_norm_rows(x_ref[...], nw_ref[...]).astype(BF16)
    for j in range(0, S5_TILES, 2):
        uu = jnp.dot(h, w_ref[:, j * LANES:(j + 2) * LANES], preferred_element_type=F32)
        u_ref[j] = uu[:, :LANES]
        u_ref[j + 1] = uu[:, LANES:]
    _mm_store(h, w_ref, SEQ_WIDTH, SEQ_WIDTH + MIX_WIDTH, z_ref)
    _mm_store(h, w_ref, SEQ_WIDTH + MIX_WIDTH, SEQ_WIDTH + MIX_WIDTH + XA_WIDTH, xq_ref)


def _s5_proj(x2, norm_w, w_in):
    tok = x2.shape[0]
    tm = PROJ_ROWS
    n_in = w_in.shape[1]
    row = lambda w: pl.BlockSpec((tm, w), lambda i: (i, 0))
    return pl.pallas_call(
        _s5_proj_kernel,
        grid=(tok // tm,),
        in_specs=[row(D_MODEL), _const_spec((1, D_MODEL)), _const_spec((D_MODEL, n_in))],
        out_specs=[pl.BlockSpec((S5_TILES, tm, LANES), lambda i: (0, i, 0)), row(MIX_WIDTH), row(XA_WIDTH)],
        out_shape=[jax.ShapeDtypeStruct((S5_TILES, tok, LANES), F32),
                   jax.ShapeDtypeStruct((tok, MIX_WIDTH), BF16), jax.ShapeDtypeStruct((tok, XA_WIDTH), BF16)],
        compiler_params=_params(1),
        name="s5_proj",
    )(x2, norm_w.reshape(1, D_MODEL), w_in.astype(BF16))


def _s5_operators(lam_re, lam_im, log_step, b_re, b_im, c_re, c_im):
    ng, t, p = S5_TILE_GROUPS, S5_T, S5_STATE
    grp = lambda tail: pl.BlockSpec((ng,) + tail, lambda j: (j,) + (0,) * len(tail))
    f32 = lambda tail: jax.ShapeDtypeStruct((S5_GROUPS,) + tail, F32)
    conv, tost_re, tost_im, from_re, from_im, pow_re, pow_im = pl.pallas_call(
        _s5_ops_kernel,
        grid=(S5_TILES,),
        in_specs=[grp((1, p)), grp((1, p)), grp((1, 1)), grp((S5_GROUP, p)), grp((S5_GROUP, p)),
                  grp((S5_GROUP, p)), grp((S5_GROUP, p))],
        out_specs=[grp((S5_K, S5_K)), grp((S5_K, p)), grp((S5_K, p)), grp((S5_K, p)), grp((S5_K, p)),
                   grp((SUBLANES, p)), grp((SUBLANES, p))],
        out_shape=[jax.ShapeDtypeStruct((S5_GROUPS, S5_K, S5_K), BF16), f32((S5_K, p)), f32((S5_K, p)),
                   f32((S5_K, p)), f32((S5_K, p)), f32((SUBLANES, p)), f32((SUBLANES, p))],
        compiler_params=_params(1),
        name="s5_ops",
    )(lam_re.reshape(S5_GROUPS, 1, p), lam_im.reshape(S5_GROUPS, 1, p), log_step.reshape(S5_GROUPS, 1, 1),
      jnp.transpose(b_re, (0, 2, 1)), jnp.transpose(b_im, (0, 2, 1)), c_re, c_im)
    to_st = jnp.transpose(jnp.concatenate([tost_re, tost_im], axis=-1), (0, 2, 1)).astype(BF16)
    from_st = jnp.concatenate([from_re, from_im], axis=-1).astype(BF16)
    tiles = lambda a: a.reshape(S5_TILES, ng, SUBLANES, p).transpose(0, 2, 1, 3).reshape(
        S5_TILES, SUBLANES, ng * p)
    tile4 = lambda a: a.reshape((S5_TILES, ng) + a.shape[1:])
    return tile4(conv), tile4(to_st), tile4(from_st), tiles(pow_re), tiles(pow_im)


def _s5_ops_kernel(lre_ref, lim_ref, ls_ref, btr_ref, bti_ref, cr_ref, ci_ref,
                   conv_ref, tsr_ref, tsi_ref, fr_ref, fi_ref, pwr_ref, pwi_ref):
    t, hh, p = S5_T, S5_GROUP, S5_STATE
    hp = lax.Precision.HIGHEST
    m_rows = lax.broadcasted_iota(jnp.int32, (3 * SUBLANES, p), 0).astype(F32)
    j_rows = (lax.broadcasted_iota(jnp.int32, (SUBLANES, p), 0) + 1).astype(F32) * t
    lane_s = lax.broadcasted_iota(jnp.int32, (hh, S5_K), 1) // hh
    for g in range(S5_TILE_GROUPS):
        lre, lim = lre_ref[g], lim_ref[g]
        dt = jnp.exp(ls_ref[g])
        mag = jnp.exp(lre * dt * m_rows)
        ang = lim * dt * m_rows
        pw_re, pw_im = mag * jnp.cos(ang), mag * jnp.sin(ang)
        magj = jnp.exp(lre * dt * j_rows)
        angj = lim * dt * j_rows
        pwr_ref[g] = magj * jnp.cos(angj)
        pwi_ref[g] = magj * jnp.sin(angj)
        den = lre * lre + lim * lim
        nr, ni = pw_re[1:2] - 1.0, pw_im[1:2]
        coef_re = (nr * lre + ni * lim) / den
        coef_im = (ni * lre - nr * lim) / den
        bt_re, bt_im = btr_ref[g], bti_ref[g]
        bb_re = coef_re * bt_re - coef_im * bt_im
        bb_im = coef_re * bt_im + coef_im * bt_re
        c_re, c_im = cr_ref[g], ci_ref[g]
        ca_re = [c_re * pw_re[m:m + 1] - c_im * pw_im[m:m + 1] for m in range(t + 1)]
        ca_im = [c_re * pw_im[m:m + 1] + c_im * pw_re[m:m + 1] for m in range(t + 1)]
        fr_ref[g] = jnp.concatenate(ca_re[1:], axis=0)
        fi_ref[g] = -jnp.concatenate(ca_im[1:], axis=0)
        kwide = (lax.dot_general(jnp.concatenate(ca_re[:t], axis=0), jnp.concatenate([bb_re] * t, axis=0),
                                 (((1,), (1,)), ((), ())), precision=hp, preferred_element_type=F32)
                 - lax.dot_general(jnp.concatenate(ca_im[:t], axis=0), jnp.concatenate([bb_im] * t, axis=0),
                                   (((1,), (1,)), ((), ())), precision=hp, preferred_element_type=F32))
        for ti in range(t):
            blk = jnp.zeros((hh, S5_K), F32)
            for m in range(ti + 1):
                blk = jnp.where(lane_s == ti - m, kwide[m * hh:(m + 1) * hh, :], blk)
            conv_ref[g, ti * hh:(ti + 1) * hh, :] = blk.astype(BF16)
        for s in range(t):
            wr, wi = pw_re[t - 1 - s:t - s], pw_im[t - 1 - s:t - s]
            tsr_ref[g, s * hh:(s + 1) * hh, :] = wr * bb_re - wi * bb_im
            tsi_ref[g, s * hh:(s + 1) * hh, :] = wr * bb_im + wi * bb_re


def _s5_core_kernel(u_ref, conv_ref, tost_ref, fromst_ref, pr_ref, pi_ref, y_ref,
                    cr_ref, ci_ref, zr_ref, zi_ref, sr_ref, si_ref, r_ref, *, n):
    @pl.when(pl.program_id(2) == 0)
    def _():
        cr_ref[...] = jnp.zeros_like(cr_ref)
        ci_ref[...] = jnp.zeros_like(ci_ref)

    half = S5_STATE
    ng = S5_TILE_GROUPS
    at = [u_ref[0, pl.ds(s, n, stride=S5_T), :].T for s in range(S5_T)]
    z = []
    for g in range(ng):
        rg = jnp.concatenate([a[g * S5_GROUP:(g + 1) * S5_GROUP, :] for a in at], axis=0).astype(BF16)
        r_ref[g] = rg
        z.append(jnp.dot(tost_ref[0, g], rg, preferred_element_type=F32))
    for q in range(ng // 2):
        ls = slice(q * LANES, (q + 1) * LANES)
        zr_ref[:, ls] = jnp.concatenate([z[2 * q][:half], z[2 * q + 1][:half]], axis=0).T
        zi_ref[:, ls] = jnp.concatenate([z[2 * q][half:], z[2 * q + 1][half:]], axis=0).T

    pr = pr_ref[0]
    pi_ = pi_ref[0]
    width = ng * half
    rows = lax.broadcasted_iota(jnp.int32, (SUBLANES, width), 0)

    def cmul_add(xr, xi, ar, ai, br, bi):
        return xr + ar * br - ai * bi, xi + ar * bi + ai * br

    def tile_step(it, carry):
        c_r, c_i = carry
        r0 = pl.multiple_of(it * SUBLANES, SUBLANES)
        xr = zr_ref[pl.ds(r0, SUBLANES), :]
        xi = zi_ref[pl.ds(r0, SUBLANES), :]
        for k in (1, 2, 4):
            keep = rows >= k
            shr = jnp.where(keep, pltpu.roll(xr, k, 0), 0.0)
            shi = jnp.where(keep, pltpu.roll(xi, k, 0), 0.0)
            xr, xi = cmul_add(xr, xi, pr[k - 1:k, :], pi_[k - 1:k, :], shr, shi)
        s_r, s_i = cmul_add(xr, xi, pr, pi_, c_r, c_i)
        first = rows == 0
        sr_ref[pl.ds(r0, SUBLANES), :] = jnp.where(first, c_r, pltpu.roll(s_r, 1, 0))
        si_ref[pl.ds(r0, SUBLANES), :] = jnp.where(first, c_i, pltpu.roll(s_i, 1, 0))
        return (jnp.broadcast_to(s_r[SUBLANES - 1:, :], (SUBLANES, width)),
                jnp.broadcast_to(s_i[SUBLANES - 1:, :], (SUBLANES, width)))

    c_r, c_i = lax.fori_loop(0, n // SUBLANES, tile_step, (cr_ref[...], ci_ref[...]))
    cr_ref[...] = c_r
    ci_ref[...] = c_i

    yt = []
    for q in range(ng // 2):
        ls = slice(q * LANES, (q + 1) * LANES)
        srt = sr_ref[:, ls].T
        sit = si_ref[:, ls].T
        for e in range(2):
            g = 2 * q + e
            hs = slice(e * half, (e + 1) * half)
            sprev = jnp.concatenate([srt[hs], sit[hs]], axis=0).astype(BF16)
            yt.append(jnp.dot(conv_ref[0, g], r_ref[g], preferred_element_type=F32)
                      + jnp.dot(fromst_ref[0, g], sprev, preferred_element_type=F32))
    for t in range(S5_T):
        blk = jnp.concatenate([y[t * S5_GROUP:(t + 1) * S5_GROUP, :] for y in yt], axis=0)
        y_ref[0, pl.ds(t, n, stride=S5_T), :] = blk.T


def _s5_core(u3, ops, batch, seq):
    conv, to_st, from_st, pow_re, pow_im = ops
    blk = min(S5_BLOCK, seq)
    nblk = seq // blk
    n = blk // S5_T
    width = S5_TILE_GROUPS * S5_STATE
    tok_spec = pl.BlockSpec((1, blk, LANES), lambda j, b, i: (j, b * nblk + i, 0))
    tile_spec = lambda s: pl.BlockSpec((1,) + s, lambda j, b, i: (j,) + (0,) * len(s))
    return pl.pallas_call(
        functools.partial(_s5_core_kernel, n=n),
        grid=(S5_TILES, batch, nblk),
        in_specs=[tok_spec, tile_spec((S5_TILE_GROUPS, S5_K, S5_K)),
                  tile_spec((S5_TILE_GROUPS, 2 * S5_STATE, S5_K)),
                  tile_spec((S5_TILE_GROUPS, S5_K, 2 * S5_STATE)),
                  tile_spec((SUBLANES, width)), tile_spec((SUBLANES, width))],
        out_specs=tok_spec,
        out_shape=jax.ShapeDtypeStruct(u3.shape, F32),
        scratch_shapes=[pltpu.VMEM((SUBLANES, width), F32), pltpu.VMEM((SUBLANES, width), F32),
                        pltpu.VMEM((n, width), F32), pltpu.VMEM((n, width), F32),
                        pltpu.VMEM((n, width), F32), pltpu.VMEM((n, width), F32),
                        pltpu.VMEM((S5_TILE_GROUPS, S5_K, n), BF16)],
        compiler_params=_params(3),
        name="s5_core",
    )(u3, conv, to_st, from_st, pow_re, pow_im)


def _gelu_tanh(x):
    return 0.5 * x * (1.0 + jnp.tanh(math.sqrt(2.0 / math.pi) * (x + 0.044715 * (x * x * x))))


def _s5_post_kernel(y_ref, u_ref, d_ref, wg_ref, bg_ref, o_ref, act_ref):
    for j in range(S5_TILES):
        ls = slice(j * LANES, (j + 1) * LANES)
        act_ref[:, ls] = _gelu_tanh(y_ref[j] + d_ref[:, ls] * u_ref[j])
    act = act_ref[...]
    gate = jax.nn.sigmoid(jnp.dot(act.astype(BF16), wg_ref[...], preferred_element_type=F32) + bg_ref[...])
    o_ref[...] = (act * gate).astype(o_ref.dtype)


def _s5_post(y3, u3, d, w_glu, b_glu):
    tok = y3.shape[1]
    tm = PROJ_ROWS
    t3 = pl.BlockSpec((S5_TILES, tm, LANES), lambda i: (0, i, 0))
    return pl.pallas_call(
        _s5_post_kernel,
        grid=(tok // tm,),
        in_specs=[t3, t3, _const_spec((1, SEQ_WIDTH)), _const_spec((SEQ_WIDTH, SEQ_WIDTH)),
                  _const_spec((1, SEQ_WIDTH))],
        out_specs=pl.BlockSpec((tm, SEQ_WIDTH), lambda i: (i, 0)),
        out_shape=jax.ShapeDtypeStruct((tok, SEQ_WIDTH), BF16),
        scratch_shapes=[pltpu.VMEM((tm, SEQ_WIDTH), F32)],
        compiler_params=_params(1),
        name="s5_post",
    )(y3, u3, d.reshape(1, SEQ_WIDTH), w_glu.astype(BF16), b_glu.reshape(1, SEQ_WIDTH))


def _mix_kernel(seq_ref, z_ref, xq_ref, k_ref, v_ref, wo_ref, x_ref, fw_ref, o_ref, *, final):
    z = z_ref[...].astype(F32)
    gate = z * jax.nn.sigmoid(z)
    y_seq = (seq_ref[...].astype(F32) * gate[:, :SEQ_WIDTH]).astype(BF16)
    acc = x_ref[...] + jnp.dot(y_seq, wo_ref[:SEQ_WIDTH, :], preferred_element_type=F32)
    for h in range(XA_HEADS):
        hs = slice(h * XA_HEAD_DIM, (h + 1) * XA_HEAD_DIM)
        s = _dot_nt(xq_ref[:, hs], k_ref[0, :, hs]) * (XA_HEAD_DIM ** -0.5)
        s = s - jnp.max(s, axis=-1, keepdims=True)
        p = jnp.exp(s)
        p = (p / jnp.sum(p, axis=-1, keepdims=True)).astype(BF16)
        xa = jnp.dot(p, v_ref[0, :, hs], preferred_element_type=F32)
        y_xa = (xa * gate[:, SEQ_WIDTH + h * XA_HEAD_DIM:SEQ_WIDTH + (h + 1) * XA_HEAD_DIM]).astype(BF16)
        acc = acc + jnp.dot(y_xa, wo_ref[SEQ_WIDTH + h * XA_HEAD_DIM:SEQ_WIDTH + (h + 1) * XA_HEAD_DIM, :],
                            preferred_element_type=F32)
    if final:
        acc = _rms_norm_rows(acc, fw_ref[...])
    o_ref[...] = acc


def _mix(seq_out, z, xq, kv, layer, w_out, x2, final_w, batch, seq, final):
    tok = x2.shape[0]
    tm = min(PROJ_ROWS, seq)
    per_b = seq // tm
    row = lambda w: pl.BlockSpec((tm, w), lambda i: (i, 0))
    n_mem = kv.shape[1] // batch
    k_spec = pl.BlockSpec((1, n_mem, XA_WIDTH), lambda i: (layer, i // per_b, 0))
    v_spec = pl.BlockSpec((1, n_mem, XA_WIDTH), lambda i: (layer, i // per_b, 1))
    return pl.pallas_call(
        functools.partial(_mix_kernel, final=final),
        grid=(tok // tm,),
        in_specs=[row(SEQ_WIDTH), row(MIX_WIDTH), row(XA_WIDTH), k_spec, v_spec,
                  _const_spec((MIX_WIDTH, D_MODEL)), row(D_MODEL), _const_spec((1, D_MODEL))],
        out_specs=row(D_MODEL),
        out_shape=jax.ShapeDtypeStruct((tok, D_MODEL), F32),
        compiler_params=_params(1),
        name="mix_final" if final else "mix",
    )(seq_out, z, xq, kv, kv, w_out.astype(BF16), x2, final_w.reshape(1, D_MODEL))


def kernel(x, mem, norm_w, mem_norm_w, gla_w_in, gla_w_gate_up, gla_gate_bias, gla_out_norm_w, s5_w_in, s5_lam_re, s5_lam_im, s5_log_step, s5_b_re, s5_b_im, s5_c_re, s5_c_im, s5_d, s5_w_glu, s5_b_glu, xa_w_kv, w_out, final_norm_w):
    batch, seq, _ = x.shape
    depth = norm_w.shape[0]
    x2 = x.reshape(batch * seq, D_MODEL)
    kv = _mem_kv(mem.reshape(-1, D_MODEL), mem_norm_w, xa_w_kv.astype(BF16))
    for i in range(depth):
        j = i // 2
        if i % 2 == 0:
            q, k, v, g, z, xq = _gla_proj(x2, norm_w[i], gla_w_in[j], gla_w_gate_up[j], gla_gate_bias[j])
            seq_out = _gla(q, k, v, g, gla_out_norm_w[j], batch, seq)
        else:
            u3, z, xq = _s5_proj(x2, norm_w[i], s5_w_in[j])
            ops = _s5_operators(s5_lam_re[j], s5_lam_im[j], s5_log_step[j], s5_b_re[j], s5_b_im[j],
                                s5_c_re[j], s5_c_im[j])
            y3 = _s5_core(u3, ops, batch, seq)
            seq_out = _s5_post(y3, u3, s5_d[j], s5_w_glu[j], s5_b_glu[j])
        x2 = _mix(seq_out, z, xq, kv, i, w_out[i], x2, final_norm_w, batch, seq, final=(i == depth - 1))
    return x2.reshape(batch, seq, D_MODEL)
```

```python
import functools
import math

import jax
import jax.numpy as jnp
from jax import lax
from jax.experimental import pallas as pl
from jax.experimental.pallas import tpu as pltpu

F32 = jnp.float32
BF16 = jnp.bfloat16

D_MODEL = 1024
EPS = 1e-6
SEQ_WIDTH = 2 * D_MODEL
XA_HEADS = 4
XA_HEAD_DIM = D_MODEL // XA_HEADS
XA_WIDTH = D_MODEL
MIX_WIDTH = SEQ_WIDTH + XA_WIDTH
GLA_HEADS = 4
GLA_DK = 128
GLA_DV = 512
GLA_QK = GLA_HEADS * GLA_DK
GLA_RANK = 16
GLA_TAU = 16.0
GLA_CHUNK = 64
S5_GROUP = 16
S5_GROUPS = SEQ_WIDTH // S5_GROUP
S5_STATE = 64

LANES = 128
SUBLANES = 8
S5_T = 16
S5_TILE_GROUPS = LANES // S5_GROUP
S5_TILES = SEQ_WIDTH // LANES
S5_K = S5_T * S5_GROUP

PROJ_ROWS = 512
GLA_ROWS = 512
S5_BLOCK = 4096
MIX_CHUNK = 512
MIX_ROWS = 1024
VMEM_LIMIT = 56 * 1024 * 1024


def _const_spec(shape):
    nd = len(shape)
    return pl.BlockSpec(shape, lambda *_: (0,) * nd, pipeline_mode=pl.Buffered(1))


def _params(n_axes):
    return pltpu.CompilerParams(dimension_semantics=("arbitrary",) * n_axes,
                                vmem_limit_bytes=VMEM_LIMIT)


def _rms_norm_rows(x, w):
    ms = jnp.mean(x * x, axis=-1, keepdims=True)
    return x * lax.rsqrt(ms + EPS) * w


def _mm_store(h, w_ref, lo, hi, out_ref, chunk=512):
    for c in range(lo, hi, chunk):
        out_ref[:, c - lo:c - lo + chunk] = jnp.dot(
            h, w_ref[:, c:c + chunk], preferred_element_type=F32).astype(out_ref.dtype)


def _mem_kv_kernel(mem_ref, nw_ref, w_ref, kv_ref):
    mn = _rms_norm_rows(mem_ref[...], nw_ref[...]).astype(BF16)
    kv_ref[0] = jnp.dot(mn, w_ref[0], preferred_element_type=F32).astype(BF16)


def _mem_kv(mem2, mem_norm_w, w_kv_bf):
    depth = w_kv_bf.shape[0]
    rows, n_mem = mem2.shape[0], 256
    nb = rows // n_mem
    return pl.pallas_call(
        _mem_kv_kernel,
        grid=(depth, nb),
        in_specs=[pl.BlockSpec((n_mem, D_MODEL), lambda i, b: (b, 0)),
                  pl.BlockSpec((1, D_MODEL), lambda i, b: (0, 0)),
                  pl.BlockSpec((1, D_MODEL, 2 * XA_WIDTH), lambda i, b: (i, 0, 0))],
        out_specs=pl.BlockSpec((1, n_mem, 2 * XA_WIDTH), lambda i, b: (i, b, 0)),
        out_shape=jax.ShapeDtypeStruct((depth, rows, 2 * XA_WIDTH), BF16),
        compiler_params=_params(2),
        name="mem_kv",
    )(mem2, mem_norm_w.reshape(1, D_MODEL), w_kv_bf)


def _log_sigmoid(x):
    return jnp.minimum(x, 0.0) - jnp.log(1.0 + jnp.exp(-jnp.abs(x)))


def _gla_proj_kernel(x_ref, nw_ref, w_ref, wr_ref, wg_ref, gb_ref,
                     q_ref, k_ref, v_ref, g_ref, z_ref, xq_ref):
    h = _rms_norm_rows(x_ref[...], nw_ref[...]).astype(BF16)
    _mm_store(h, w_ref, 0, GLA_QK, q_ref)
    _mm_store(h, w_ref, GLA_QK, 2 * GLA_QK, k_ref)
    _mm_store(h, w_ref, 2 * GLA_QK, 2 * GLA_QK + SEQ_WIDTH, v_ref)
    z0 = 2 * GLA_QK + SEQ_WIDTH
    _mm_store(h, w_ref, z0, z0 + MIX_WIDTH, z_ref)
    _mm_store(h, w_ref, z0 + MIX_WIDTH, z0 + MIX_WIDTH + XA_WIDTH, xq_ref)
    r = jnp.dot(h, wr_ref[...], preferred_element_type=F32).astype(BF16)
    pre = jnp.dot(r, wg_ref[...], preferred_element_type=F32) + gb_ref[...]
    g_ref[...] = _log_sigmoid(pre) * (1.0 / GLA_TAU)


def _gla_proj(x2, norm_w, w_in, w_gate_up, gate_bias):
    tok = x2.shape[0]
    tm = PROJ_ROWS
    qk = GLA_QK
    r0 = 2 * qk + SEQ_WIDTH
    w_main = jnp.concatenate([w_in[:, :r0], w_in[:, r0 + GLA_RANK:]], axis=1).astype(BF16)
    w_r = jnp.pad(w_in[:, r0:r0 + GLA_RANK], ((0, 0), (0, LANES - GLA_RANK))).astype(BF16)
    w_g = jnp.pad(w_gate_up, ((0, LANES - GLA_RANK), (0, 0))).astype(BF16)
    n_main = w_main.shape[1]
    row = lambda w: pl.BlockSpec((tm, w), lambda i: (i, 0))
    return pl.pallas_call(
        _gla_proj_kernel,
        grid=(tok // tm,),
        in_specs=[row(D_MODEL), _const_spec((1, D_MODEL)), _const_spec((D_MODEL, n_main)),
                  _const_spec((D_MODEL, LANES)), _const_spec((LANES, qk)), _const_spec((1, qk))],
        out_specs=[row(qk), row(qk), row(SEQ_WIDTH), row(qk), row(MIX_WIDTH), row(XA_WIDTH)],
        out_shape=[jax.ShapeDtypeStruct((tok, qk), BF16), jax.ShapeDtypeStruct((tok, qk), BF16),
                   jax.ShapeDtypeStruct((tok, SEQ_WIDTH), BF16), jax.ShapeDtypeStruct((tok, qk), F32),
                   jax.ShapeDtypeStruct((tok, MIX_WIDTH), BF16), jax.ShapeDtypeStruct((tok, XA_WIDTH), BF16)],
        compiler_params=_params(1),
        name="gla_proj",
    )(x2, norm_w.reshape(1, D_MODEL), w_main, w_r, w_g, gate_bias.reshape(1, qk))


def _dot_nt(a, b):
    return lax.dot_general(a, b, (((1,), (1,)), ((), ())), preferred_element_type=F32)


def _dot_tn(a, b):
    return lax.dot_general(a, b, (((0,), (0,)), ((), ())), preferred_element_type=F32)


def _gla_kernel(q_ref, k_ref, v_ref, g_ref, onw_ref, o_ref, st_ref, qd_ref, el_ref, att_ref, kv_ref, *, rows):
    @pl.when(pl.program_id(1) == 0)
    def _():
        st_ref[...] = jnp.zeros_like(st_ref)

    c = GLA_CHUNK
    ri = lax.broadcasted_iota(jnp.int32, (c, c), 0)
    ci = lax.broadcasted_iota(jnp.int32, (c, c), 1)
    causal = ri >= ci
    tri = causal.astype(BF16)
    scale = GLA_DK ** -0.5
    onw = onw_ref[...]
    n_chunks = rows // c

    for ic in range(n_chunks):
        rs = slice(ic * c, (ic + 1) * c)
        gc = g_ref[rs, :]
        g_hi = gc.astype(BF16)
        g_lo = (gc - g_hi.astype(F32)).astype(BF16)
        bcum = (jnp.dot(tri, g_hi, preferred_element_type=F32)
                + jnp.dot(tri, g_lo, preferred_element_type=F32))
        b_last = bcum[c - 1:c, :]
        qc = q_ref[rs, :].astype(F32)
        kc = k_ref[rs, :].astype(F32)
        q_dec = (qc * (jnp.exp(bcum) * scale)).astype(BF16)
        k_inv = (kc * jnp.exp(-bcum)).astype(BF16)
        k_end = (kc * jnp.exp(b_last - bcum)).astype(BF16)
        qd_ref[rs, :] = q_dec
        el_ref[ic:ic + 1, :] = jnp.exp(b_last)
        for h in range(GLA_HEADS):
            ks = slice(h * GLA_DK, (h + 1) * GLA_DK)
            vs = slice(h * GLA_DV, (h + 1) * GLA_DV)
            att_ref[ic, h] = jnp.where(causal, _dot_nt(q_dec[:, ks], k_inv[:, ks]), 0.0).astype(BF16)
            kv_ref[ic, h] = _dot_tn(v_ref[rs, vs], k_end[:, ks])

    for ic in range(n_chunks):
        rs = slice(ic * c, (ic + 1) * c)
        e_last = el_ref[ic:ic + 1, :]
        for h in range(GLA_HEADS):
            ks = slice(h * GLA_DK, (h + 1) * GLA_DK)
            vs = slice(h * GLA_DV, (h + 1) * GLA_DV)
            st = st_ref[h]
            o = (jnp.dot(att_ref[ic, h], v_ref[rs, vs], preferred_element_type=F32)
                 + _dot_nt(qd_ref[rs, ks], st.astype(BF16)))
            st_ref[h] = st * e_last[:, ks] + kv_ref[ic, h]
            ms = jnp.mean(o * o, axis=-1, keepdims=True)
            o_ref[rs, vs] = (o * lax.rsqrt(ms + EPS) * onw).astype(o_ref.dtype)


def _gla(q, k, v, g, out_norm_w, batch, seq):
    rows = min(GLA_ROWS, seq)
    nblk = seq // rows
    n_chunks = rows // GLA_CHUNK
    row = lambda w: pl.BlockSpec((rows, w), lambda b, i: (b * nblk + i, 0))
    return pl.pallas_call(
        functools.partial(_gla_kernel, rows=rows),
        grid=(batch, nblk),
        in_specs=[row(GLA_QK), row(GLA_QK), row(SEQ_WIDTH), row(GLA_QK), _const_spec((1, GLA_DV))],
        out_specs=row(SEQ_WIDTH),
        out_shape=jax.ShapeDtypeStruct((batch * seq, SEQ_WIDTH), BF16),
        scratch_shapes=[pltpu.VMEM((GLA_HEADS, GLA_DV, GLA_DK), F32),
                        pltpu.VMEM((rows, GLA_QK), BF16),
                        pltpu.VMEM((n_chunks, GLA_QK), F32),
                        pltpu.VMEM((n_chunks, GLA_HEADS, GLA_CHUNK, GLA_CHUNK), BF16),
                        pltpu.VMEM((n_chunks, GLA_HEADS, GLA_DV, GLA_DK), F32)],
        compiler_params=_params(2),
        name="gla_scan",
    )(q, k, v, g, out_norm_w.reshape(1, GLA_DV))


def _s5_proj_kernel(x_ref, nw_ref, w_ref, u_ref, z_ref, xq_ref):
    h = _rms_norm_rows(x_ref[...], nw_ref[...]).astype(BF16)
    for j in range(0, S5_TILES, 2):
        uu = jnp.dot(h, w_ref[:, j * LANES:(j + 2) * LANES], preferred_element_type=F32).astype(BF16)
        u_ref[j] = uu[:, :LANES]
        u_ref[j + 1] = uu[:, LANES:]
    _mm_store(h, w_ref, SEQ_WIDTH, SEQ_WIDTH + MIX_WIDTH, z_ref)
    _mm_store(h, w_ref, SEQ_WIDTH + MIX_WIDTH, SEQ_WIDTH + MIX_WIDTH + XA_WIDTH, xq_ref)


def _chunk_rows(seq):
    return min(PROJ_ROWS, seq // S5_T)


def _natural_rows_spec(rows, width):
    return pl.BlockSpec((rows, width), lambda i, s: (i, s))


def _offset_major_spec(rows, width):
    return pl.BlockSpec((None, rows, width), lambda i, s: (s, i, 0))


def _tile_offset_major_spec(rows):
    return pl.BlockSpec((S5_TILES, None, rows, LANES), lambda i, s: (0, s, i, 0))


def _s5_proj(x2, norm_w, w_in, seq):
    chunks = x2.shape[0] // S5_T
    rows = _chunk_rows(seq)
    n_in = w_in.shape[1]
    return pl.pallas_call(
        _s5_proj_kernel,
        grid=(chunks // rows, S5_T),
        in_specs=[_natural_rows_spec(rows, D_MODEL), _const_spec((1, D_MODEL)), _const_spec((D_MODEL, n_in))],
        out_specs=[_tile_offset_major_spec(rows), _offset_major_spec(rows, MIX_WIDTH),
                   _offset_major_spec(rows, XA_WIDTH)],
        out_shape=[jax.ShapeDtypeStruct((S5_TILES, S5_T, chunks, LANES), BF16),
                   jax.ShapeDtypeStruct((S5_T, chunks, MIX_WIDTH), BF16),
                   jax.ShapeDtypeStruct((S5_T, chunks, XA_WIDTH), BF16)],
        compiler_params=_params(2),
        name="s5_proj",
    )(x2.reshape(chunks, S5_T * D_MODEL), norm_w.reshape(1, D_MODEL), w_in.astype(BF16))


def _s5_operators(lam_re, lam_im, log_step, b_re, b_im, c_re, c_im):
    ng, t, p = S5_TILE_GROUPS, S5_T, S5_STATE
    grp = lambda tail: pl.BlockSpec((ng,) + tail, lambda j: (j,) + (0,) * len(tail))
    f32 = lambda tail: jax.ShapeDtypeStruct((S5_GROUPS,) + tail, F32)
    conv, tost_re, tost_im, from_re, from_im, pow_re, pow_im = pl.pallas_call(
        _s5_ops_kernel,
        grid=(S5_TILES,),
        in_specs=[grp((1, p)), grp((1, p)), grp((1, 1)), grp((S5_GROUP, p)), grp((S5_GROUP, p)),
                  grp((S5_GROUP, p)), grp((S5_GROUP, p))],
        out_specs=[grp((S5_K, S5_K)), grp((S5_K, p)), grp((S5_K, p)), grp((S5_K, p)), grp((S5_K, p)),
                   grp((SUBLANES, p)), grp((SUBLANES, p))],
        out_shape=[jax.ShapeDtypeStruct((S5_GROUPS, S5_K, S5_K), BF16), f32((S5_K, p)), f32((S5_K, p)),
                   f32((S5_K, p)), f32((S5_K, p)), f32((SUBLANES, p)), f32((SUBLANES, p))],
        compiler_params=_params(1),
        name="s5_ops",
    )(lam_re.reshape(S5_GROUPS, 1, p), lam_im.reshape(S5_GROUPS, 1, p), log_step.reshape(S5_GROUPS, 1, 1),
      jnp.transpose(b_re, (0, 2, 1)), jnp.transpose(b_im, (0, 2, 1)), c_re, c_im)
    to_st = jnp.transpose(jnp.concatenate([tost_re, tost_im], axis=-1), (0, 2, 1)).astype(BF16)
    from_st = jnp.concatenate([from_re, from_im], axis=-1).astype(BF16)
    tiles = lambda a: a.reshape(S5_TILES, ng, SUBLANES, p).transpose(0, 2, 1, 3).reshape(
        S5_TILES, SUBLANES, ng * p)
    tile4 = lambda a: a.reshape((S5_TILES, ng) + a.shape[1:])
    return tile4(conv), tile4(to_st), tile4(from_st), tiles(pow_re), tiles(pow_im)


def _s5_ops_kernel(lre_ref, lim_ref, ls_ref, btr_ref, bti_ref, cr_ref, ci_ref,
                   conv_ref, tsr_ref, tsi_ref, fr_ref, fi_ref, pwr_ref, pwi_ref):
    t, hh, p = S5_T, S5_GROUP, S5_STATE
    hp = lax.Precision.HIGHEST
    m_rows = lax.broadcasted_iota(jnp.int32, (3 * SUBLANES, p), 0).astype(F32)
    j_rows = (lax.broadcasted_iota(jnp.int32, (SUBLANES, p), 0) + 1).astype(F32) * t
    lane_s = lax.broadcasted_iota(jnp.int32, (hh, S5_K), 1) // hh
    for g in range(S5_TILE_GROUPS):
        lre, lim = lre_ref[g], lim_ref[g]
        dt = jnp.exp(ls_ref[g])
        mag = jnp.exp(lre * dt * m_rows)
        ang = lim * dt * m_rows
        pw_re, pw_im = mag * jnp.cos(ang), mag * jnp.sin(ang)
        magj = jnp.exp(lre * dt * j_rows)
        angj = lim * dt * j_rows
        pwr_ref[g] = magj * jnp.cos(angj)
        pwi_ref[g] = magj * jnp.sin(angj)
        den = lre * lre + lim * lim
        nr, ni = pw_re[1:2] - 1.0, pw_im[1:2]
        coef_re = (nr * lre + ni * lim) / den
        coef_im = (ni * lre - nr * lim) / den
        bt_re, bt_im = btr_ref[g], bti_ref[g]
        bb_re = coef_re * bt_re - coef_im * bt_im
        bb_im = coef_re * bt_im + coef_im * bt_re
        c_re, c_im = cr_ref[g], ci_ref[g]
        ca_re = [c_re * pw_re[m:m + 1] - c_im * pw_im[m:m + 1] for m in range(t + 1)]
        ca_im = [c_re * pw_im[m:m + 1] + c_im * pw_re[m:m + 1] for m in range(t + 1)]
        fr_ref[g] = jnp.concatenate(ca_re[1:], axis=0)
        fi_ref[g] = -jnp.concatenate(ca_im[1:], axis=0)
        kwide = (lax.dot_general(jnp.concatenate(ca_re[:t], axis=0), jnp.concatenate([bb_re] * t, axis=0),
                                 (((1,), (1,)), ((), ())), precision=hp, preferred_element_type=F32)
                 - lax.dot_general(jnp.concatenate(ca_im[:t], axis=0), jnp.concatenate([bb_im] * t, axis=0),
                                   (((1,), (1,)), ((), ())), precision=hp, preferred_element_type=F32))
        for ti in range(t):
            blk = jnp.zeros((hh, S5_K), F32)
            for m in range(ti + 1):
                blk = jnp.where(lane_s == ti - m, kwide[m * hh:(m + 1) * hh, :], blk)
            conv_ref[g, ti * hh:(ti + 1) * hh, :] = blk.astype(BF16)
        for s in range(t):
            wr, wi = pw_re[t - 1 - s:t - s], pw_im[t - 1 - s:t - s]
            tsr_ref[g, s * hh:(s + 1) * hh, :] = wr * bb_re - wi * bb_im
            tsi_ref[g, s * hh:(s + 1) * hh, :] = wr * bb_im + wi * bb_re


def _s5_core_kernel(u_ref, conv_ref, tost_ref, fromst_ref, pr_ref, pi_ref, y_ref,
                    cr_ref, ci_ref, zr_ref, zi_ref, sr_ref, si_ref, r_ref, *, n):
    @pl.when(pl.program_id(2) == 0)
    def _():
        cr_ref[...] = jnp.zeros_like(cr_ref)
        ci_ref[...] = jnp.zeros_like(ci_ref)

    half = S5_STATE
    ng = S5_TILE_GROUPS
    at = [u_ref[0, s].T for s in range(S5_T)]
    z = []
    for g in range(ng):
        rg = jnp.concatenate([a[g * S5_GROUP:(g + 1) * S5_GROUP, :] for a in at], axis=0)
        r_ref[g] = rg
        z.append(jnp.dot(tost_ref[0, g], rg, preferred_element_type=F32))
    for q in range(ng // 2):
        ls = slice(q * LANES, (q + 1) * LANES)
        zr_ref[:, ls] = jnp.concatenate([z[2 * q][:half], z[2 * q + 1][:half]], axis=0).T
        zi_ref[:, ls] = jnp.concatenate([z[2 * q][half:], z[2 * q + 1][half:]], axis=0).T

    pr = pr_ref[0]
    pi_ = pi_ref[0]
    width = ng * half
    rows = lax.broadcasted_iota(jnp.int32, (SUBLANES, width), 0)

    def cmul_add(xr, xi, ar, ai, br, bi):
        return xr + ar * br - ai * bi, xi + ar * bi + ai * br

    def tile_step(it, carry):
        c_r, c_i = carry
        r0 = pl.multiple_of(it * SUBLANES, SUBLANES)
        xr = zr_ref[pl.ds(r0, SUBLANES), :]
        xi = zi_ref[pl.ds(r0, SUBLANES), :]
        for k in (1, 2, 4):
            keep = rows >= k
            shr = jnp.where(keep, pltpu.roll(xr, k, 0), 0.0)
            shi = jnp.where(keep, pltpu.roll(xi, k, 0), 0.0)
            xr, xi = cmul_add(xr, xi, pr[k - 1:k, :], pi_[k - 1:k, :], shr, shi)
        s_r, s_i = cmul_add(xr, xi, pr, pi_, c_r, c_i)
        first = rows == 0
        sr_ref[pl.ds(r0, SUBLANES), :] = jnp.where(first, c_r, pltpu.roll(s_r, 1, 0))
        si_ref[pl.ds(r0, SUBLANES), :] = jnp.where(first, c_i, pltpu.roll(s_i, 1, 0))
        return (jnp.broadcast_to(s_r[SUBLANES - 1:, :], (SUBLANES, width)),
                jnp.broadcast_to(s_i[SUBLANES - 1:, :], (SUBLANES, width)))

    c_r, c_i = lax.fori_loop(0, n // SUBLANES, tile_step, (cr_ref[...], ci_ref[...]))
    cr_ref[...] = c_r
    ci_ref[...] = c_i

    yt = []
    for q in range(ng // 2):
        ls = slice(q * LANES, (q + 1) * LANES)
        srt = sr_ref[:, ls].T
        sit = si_ref[:, ls].T
        for e in range(2):
            g = 2 * q + e
            hs = slice(e * half, (e + 1) * half)
            sprev = jnp.concatenate([srt[hs], sit[hs]], axis=0).astype(BF16)
            yt.append(jnp.dot(conv_ref[0, g], r_ref[g], preferred_element_type=F32)
                      + jnp.dot(fromst_ref[0, g], sprev, preferred_element_type=F32))
    for t in range(S5_T):
        blk = jnp.concatenate([y[t * S5_GROUP:(t + 1) * S5_GROUP, :] for y in yt], axis=0)
        y_ref[0, t] = blk.astype(BF16).T


def _s5_core(u4, ops, batch, seq):
    conv, to_st, from_st, pow_re, pow_im = ops
    blk = min(S5_BLOCK, seq)
    nblk = seq // blk
    n = blk // S5_T
    width = S5_TILE_GROUPS * S5_STATE
    tok_spec = pl.BlockSpec((1, S5_T, n, LANES), lambda j, b, i: (j, 0, b * nblk + i, 0))
    tile_spec = lambda s: pl.BlockSpec((1,) + s, lambda j, b, i: (j,) + (0,) * len(s))
    return pl.pallas_call(
        functools.partial(_s5_core_kernel, n=n),
        grid=(S5_TILES, batch, nblk),
        in_specs=[tok_spec, tile_spec((S5_TILE_GROUPS, S5_K, S5_K)),
                  tile_spec((S5_TILE_GROUPS, 2 * S5_STATE, S5_K)),
                  tile_spec((S5_TILE_GROUPS, S5_K, 2 * S5_STATE)),
                  tile_spec((SUBLANES, width)), tile_spec((SUBLANES, width))],
        out_specs=tok_spec,
        out_shape=jax.ShapeDtypeStruct(u4.shape, BF16),
        scratch_shapes=[pltpu.VMEM((SUBLANES, width), F32), pltpu.VMEM((SUBLANES, width), F32),
                        pltpu.VMEM((n, width), F32), pltpu.VMEM((n, width), F32),
                        pltpu.VMEM((n, width), F32), pltpu.VMEM((n, width), F32),
                        pltpu.VMEM((S5_TILE_GROUPS, S5_K, n), BF16)],
        compiler_params=_params(3),
        name="s5_core",
    )(u4, conv, to_st, from_st, pow_re, pow_im)


def _gelu_tanh(x):
    return 0.5 * x * (1.0 + jnp.tanh(math.sqrt(2.0 / math.pi) * (x + 0.044715 * (x * x * x))))


def _s5_post_kernel(y_ref, u_ref, d_ref, wg_ref, bg_ref, o_ref, act_ref):
    pre = bg_ref[...]
    tiles_per_chunk = MIX_CHUNK // LANES
    for j0 in range(0, S5_TILES, tiles_per_chunk):
        for j in range(j0, j0 + tiles_per_chunk):
            ls = slice(j * LANES, (j + 1) * LANES)
            act_ref[:, ls] = _gelu_tanh(y_ref[j].astype(F32) + d_ref[:, ls] * u_ref[j].astype(F32))
        cs = slice(j0 * LANES, (j0 + tiles_per_chunk) * LANES)
        pre = pre + jnp.dot(act_ref[:, cs].astype(BF16), wg_ref[cs, :], preferred_element_type=F32)
    o_ref[...] = (act_ref[...] * jax.nn.sigmoid(pre)).astype(o_ref.dtype)


def _s5_post(y4, u4, d, w_glu, b_glu, seq):
    chunks = y4.shape[2]
    rows = _chunk_rows(seq)
    t4 = _tile_offset_major_spec(rows)
    return pl.pallas_call(
        _s5_post_kernel,
        grid=(chunks // rows, S5_T),
        in_specs=[t4, t4, _const_spec((1, SEQ_WIDTH)), _const_spec((SEQ_WIDTH, SEQ_WIDTH)),
                  _const_spec((1, SEQ_WIDTH))],
        out_specs=_offset_major_spec(rows, SEQ_WIDTH),
        out_shape=jax.ShapeDtypeStruct((S5_T, chunks, SEQ_WIDTH), BF16),
        scratch_shapes=[pltpu.VMEM((rows, SEQ_WIDTH), F32)],
        compiler_params=_params(2),
        name="s5_post",
    )(y4, u4, d.reshape(1, SEQ_WIDTH), w_glu.astype(BF16), b_glu.reshape(1, SEQ_WIDTH))


def _mix_kernel(seq_ref, z_ref, xq_ref, k_ref, v_ref, wo_ref, x_ref, fw_ref, o_ref, *, final):
    def silu(zc):
        zc = zc.astype(F32)
        return zc * jax.nn.sigmoid(zc)

    acc = x_ref[...]
    for c0 in range(0, SEQ_WIDTH, MIX_CHUNK):
        cs = slice(c0, c0 + MIX_CHUNK)
        y_seq = (seq_ref[:, cs].astype(F32) * silu(z_ref[:, cs])).astype(BF16)
        acc = acc + jnp.dot(y_seq, wo_ref[cs, :], preferred_element_type=F32)
    for h in range(XA_HEADS):
        hs = slice(h * XA_HEAD_DIM, (h + 1) * XA_HEAD_DIM)
        zs = slice(SEQ_WIDTH + h * XA_HEAD_DIM, SEQ_WIDTH + (h + 1) * XA_HEAD_DIM)
        s = _dot_nt(xq_ref[:, hs], k_ref[0, :, hs]) * (XA_HEAD_DIM ** -0.5)
        s = s - jnp.max(s, axis=-1, keepdims=True)
        p = jnp.exp(s)
        p = (p / jnp.sum(p, axis=-1, keepdims=True)).astype(BF16)
        xa = jnp.dot(p, v_ref[0, :, hs], preferred_element_type=F32)
        y_xa = (xa * silu(z_ref[:, zs])).astype(BF16)
        acc = acc + jnp.dot(y_xa, wo_ref[zs, :], preferred_element_type=F32)
    if final:
        acc = _rms_norm_rows(acc, fw_ref[...])
    o_ref[...] = acc


def _mix(seq_out, z, xq, kv, layer, w_out, x2, final_w, batch, seq, final, offset_major):
    tok = x2.shape[0]
    n_mem = kv.shape[1] // batch
    if offset_major:
        chunks = tok // S5_T
        tm = _chunk_rows(seq)
        per_b = (seq // S5_T) // tm
        grid = (chunks // tm, S5_T)
        act = lambda w: _offset_major_spec(tm, w)
        res = _natural_rows_spec(tm, D_MODEL)
        x_in = x2.reshape(chunks, S5_T * D_MODEL)
    else:
        tm = min(MIX_ROWS, seq)
        per_b = seq // tm
        grid = (tok // tm,)
        act = lambda w: pl.BlockSpec((tm, w), lambda i: (i, 0))
        res = act(D_MODEL)
        x_in = x2
    k_spec = pl.BlockSpec((1, n_mem, XA_WIDTH), lambda i, *_: (layer, i // per_b, 0))
    v_spec = pl.BlockSpec((1, n_mem, XA_WIDTH), lambda i, *_: (layer, i // per_b, 1))
    out = pl.pallas_call(
        functools.partial(_mix_kernel, final=final),
        grid=grid,
        in_specs=[act(SEQ_WIDTH), act(MIX_WIDTH), act(XA_WIDTH), k_spec, v_spec,
                  _const_spec((MIX_WIDTH, D_MODEL)), res, _const_spec((1, D_MODEL))],
        out_specs=res,
        out_shape=jax.ShapeDtypeStruct(x_in.shape, F32),
        compiler_params=_params(len(grid)),
        name="mix_final" if final else "mix",
    )(seq_out, z, xq, kv, kv, w_out.astype(BF16), x_in, final_w.reshape(1, D_MODEL))
    return out.reshape(tok, D_MODEL)


def kernel(x, mem, norm_w, mem_norm_w, gla_w_in, gla_w_gate_up, gla_gate_bias, gla_out_norm_w, s5_w_in, s5_lam_re, s5_lam_im, s5_log_step, s5_b_re, s5_b_im, s5_c_re, s5_c_im, s5_d, s5_w_glu, s5_b_glu, xa_w_kv, w_out, final_norm_w):
    batch, seq, _ = x.shape
    depth = norm_w.shape[0]
    x2 = x.reshape(batch * seq, D_MODEL)
    kv = _mem_kv(mem.reshape(-1, D_MODEL), mem_norm_w, xa_w_kv.astype(BF16))
    for i in range(depth):
        j = i // 2
        if i % 2 == 0:
            q, k, v, g, z, xq = _gla_proj(x2, norm_w[i], gla_w_in[j], gla_w_gate_up[j], gla_gate_bias[j])
            seq_out = _gla(q, k, v, g, gla_out_norm_w[j], batch, seq)
        else:
            u4, z, xq = _s5_proj(x2, norm_w[i], s5_w_in[j], seq)
            ops = _s5_operators(s5_lam_re[j], s5_lam_im[j], s5_log_step[j], s5_b_re[j], s5_b_im[j],
                                s5_c_re[j], s5_c_im[j])
            y4 = _s5_core(u4, ops, batch, seq)
            seq_out = _s5_post(y4, u4, s5_d[j], s5_w_glu[j], s5_b_glu[j], seq)
        x2 = _mix(seq_out, z, xq, kv, i, w_out[i], x2, final_norm_w, batch, seq,
                  final=(i == depth - 1), offset_major=(i % 2 == 1))
    return x2.reshape(batch, seq, D_MODEL)
```

```python
import functools
import math

import jax
import jax.numpy as jnp
from jax import lax
from jax.experimental import pallas as pl
from jax.experimental.pallas import tpu as pltpu

F32 = jnp.float32
BF16 = jnp.bfloat16

D_MODEL = 1024
EPS = 1e-6
SEQ_WIDTH = 2 * D_MODEL
XA_HEADS = 4
XA_HEAD_DIM = D_MODEL // XA_HEADS
XA_WIDTH = D_MODEL
MIX_WIDTH = SEQ_WIDTH + XA_WIDTH
GLA_HEADS = 4
GLA_DK = 128
GLA_DV = 512
GLA_QK = GLA_HEADS * GLA_DK
GLA_RANK = 16
GLA_TAU = 16.0
GLA_CHUNK = 64
S5_GROUP = 16
S5_GROUPS = SEQ_WIDTH // S5_GROUP
S5_STATE = 64

LANES = 128
SUBLANES = 8
S5_T = 16
S5_TILE_GROUPS = LANES // S5_GROUP
S5_TILES = SEQ_WIDTH // LANES
S5_K = S5_T * S5_GROUP

PROJ_ROWS = 512
GLA_ROWS = 512
S5_BLOCK = 4096
MIX_CHUNK = 512
MIX_ROWS = 1024
VMEM_LIMIT = 56 * 1024 * 1024


def _const_spec(shape):
    nd = len(shape)
    return pl.BlockSpec(shape, lambda *_: (0,) * nd, pipeline_mode=pl.Buffered(1))


def _params(n_axes):
    return pltpu.CompilerParams(dimension_semantics=("arbitrary",) * n_axes,
                                vmem_limit_bytes=VMEM_LIMIT)


def _rms_norm_rows(x, w):
    ms = jnp.mean(x * x, axis=-1, keepdims=True)
    return x * lax.rsqrt(ms + EPS) * w


def _mm_store(h, w_ref, lo, hi, out_ref, chunk=512):
    for c in range(lo, hi, chunk):
        out_ref[:, c - lo:c - lo + chunk] = jnp.dot(
            h, w_ref[:, c:c + chunk], preferred_element_type=F32).astype(out_ref.dtype)


def _mem_kv_kernel(mem_ref, nw_ref, w_ref, kv_ref):
    mn = _rms_norm_rows(mem_ref[...], nw_ref[...]).astype(BF16)
    kv_ref[0] = jnp.dot(mn, w_ref[0], preferred_element_type=F32).astype(BF16)


def _mem_kv(mem2, mem_norm_w, w_kv_bf):
    depth = w_kv_bf.shape[0]
    rows, n_mem = mem2.shape[0], 256
    nb = rows // n_mem
    return pl.pallas_call(
        _mem_kv_kernel,
        grid=(depth, nb),
        in_specs=[pl.BlockSpec((n_mem, D_MODEL), lambda i, b: (b, 0)),
                  pl.BlockSpec((1, D_MODEL), lambda i, b: (0, 0)),
                  pl.BlockSpec((1, D_MODEL, 2 * XA_WIDTH), lambda i, b: (i, 0, 0))],
        out_specs=pl.BlockSpec((1, n_mem, 2 * XA_WIDTH), lambda i, b: (i, b, 0)),
        out_shape=jax.ShapeDtypeStruct((depth, rows, 2 * XA_WIDTH), BF16),
        compiler_params=_params(2),
        name="mem_kv",
    )(mem2, mem_norm_w.reshape(1, D_MODEL), w_kv_bf)


def _log_sigmoid(x):
    return jnp.minimum(x, 0.0) - jnp.log(1.0 + jnp.exp(-jnp.abs(x)))


def _gla_proj_kernel(x_ref, nw_ref, w_ref, wr_ref, wg_ref, gb_ref,
                     q_ref, k_ref, v_ref, g_ref, z_ref, xq_ref):
    h = _rms_norm_rows(x_ref[...], nw_ref[...]).astype(BF16)
    _mm_store(h, w_ref, 0, GLA_QK, q_ref)
    _mm_store(h, w_ref, GLA_QK, 2 * GLA_QK, k_ref)
    _mm_store(h, w_ref, 2 * GLA_QK, 2 * GLA_QK + SEQ_WIDTH, v_ref)
    z0 = 2 * GLA_QK + SEQ_WIDTH
    _mm_store(h, w_ref, z0, z0 + MIX_WIDTH, z_ref)
    _mm_store(h, w_ref, z0 + MIX_WIDTH, z0 + MIX_WIDTH + XA_WIDTH, xq_ref)
    r = jnp.dot(h, wr_ref[...], preferred_element_type=F32).astype(BF16)
    pre = jnp.dot(r, wg_ref[...], preferred_element_type=F32) + gb_ref[...]
    g_ref[...] = _log_sigmoid(pre) * (1.0 / GLA_TAU)


def _gla_proj(x2, norm_w, w_in, w_gate_up, gate_bias):
    tok = x2.shape[0]
    tm = PROJ_ROWS
    qk = GLA_QK
    r0 = 2 * qk + SEQ_WIDTH
    w_main = jnp.concatenate([w_in[:, :r0], w_in[:, r0 + GLA_RANK:]], axis=1).astype(BF16)
    w_r = jnp.pad(w_in[:, r0:r0 + GLA_RANK], ((0, 0), (0, LANES - GLA_RANK))).astype(BF16)
    w_g = jnp.pad(w_gate_up, ((0, LANES - GLA_RANK), (0, 0))).astype(BF16)
    n_main = w_main.shape[1]
    row = lambda w: pl.BlockSpec((tm, w), lambda i: (i, 0))
    return pl.pallas_call(
        _gla_proj_kernel,
        grid=(tok // tm,),
        in_specs=[row(D_MODEL), _const_spec((1, D_MODEL)), _const_spec((D_MODEL, n_main)),
                  _const_spec((D_MODEL, LANES)), _const_spec((LANES, qk)), _const_spec((1, qk))],
        out_specs=[row(qk), row(qk), row(SEQ_WIDTH), row(qk), row(MIX_WIDTH), row(XA_WIDTH)],
        out_shape=[jax.ShapeDtypeStruct((tok, qk), BF16), jax.ShapeDtypeStruct((tok, qk), BF16),
                   jax.ShapeDtypeStruct((tok, SEQ_WIDTH), BF16), jax.ShapeDtypeStruct((tok, qk), F32),
                   jax.ShapeDtypeStruct((tok, MIX_WIDTH), BF16), jax.ShapeDtypeStruct((tok, XA_WIDTH), BF16)],
        compiler_params=_params(1),
        name="gla_proj",
    )(x2, norm_w.reshape(1, D_MODEL), w_main, w_r, w_g, gate_bias.reshape(1, qk))


def _dot_nt(a, b):
    return lax.dot_general(a, b, (((1,), (1,)), ((), ())), preferred_element_type=F32)


def _dot_tn(a, b):
    return lax.dot_general(a, b, (((0,), (0,)), ((), ())), preferred_element_type=F32)


def _gla_kernel(q_ref, k_ref, v_ref, g_ref, onw_ref, o_ref, st_ref, qd_ref, el_ref, att_ref, kv_ref, *, rows):
    @pl.when(pl.program_id(1) == 0)
    def _():
        st_ref[...] = jnp.zeros_like(st_ref)

    c = GLA_CHUNK
    ri = lax.broadcasted_iota(jnp.int32, (c, c), 0)
    ci = lax.broadcasted_iota(jnp.int32, (c, c), 1)
    causal = ri >= ci
    tri = causal.astype(BF16)
    scale = GLA_DK ** -0.5
    onw = onw_ref[...]
    n_chunks = rows // c

    for ic in range(n_chunks):
        rs = slice(ic * c, (ic + 1) * c)
        gc = g_ref[rs, :]
        g_hi = gc.astype(BF16)
        g_lo = (gc - g_hi.astype(F32)).astype(BF16)
        bcum = (jnp.dot(tri, g_hi, preferred_element_type=F32)
                + jnp.dot(tri, g_lo, preferred_element_type=F32))
        b_last = bcum[c - 1:c, :]
        qc = q_ref[rs, :].astype(F32)
        kc = k_ref[rs, :].astype(F32)
        q_dec = (qc * (jnp.exp(bcum) * scale)).astype(BF16)
        k_inv = (kc * jnp.exp(-bcum)).astype(BF16)
        k_end = (kc * jnp.exp(b_last - bcum)).astype(BF16)
        qd_ref[rs, :] = q_dec
        el_ref[ic:ic + 1, :] = jnp.exp(b_last)
        for h in range(GLA_HEADS):
            ks = slice(h * GLA_DK, (h + 1) * GLA_DK)
            vs = slice(h * GLA_DV, (h + 1) * GLA_DV)
            att_ref[ic, h] = jnp.where(causal, _dot_nt(q_dec[:, ks], k_inv[:, ks]), 0.0).astype(BF16)
            kv_ref[ic, h] = _dot_tn(v_ref[rs, vs], k_end[:, ks])

    for ic in range(n_chunks):
        rs = slice(ic * c, (ic + 1) * c)
        e_last = el_ref[ic:ic + 1, :]
        for h in range(GLA_HEADS):
            ks = slice(h * GLA_DK, (h + 1) * GLA_DK)
            vs = slice(h * GLA_DV, (h + 1) * GLA_DV)
            st = st_ref[h]
            o = (jnp.dot(att_ref[ic, h], v_ref[rs, vs], preferred_element_type=F32)
                 + _dot_nt(qd_ref[rs, ks], st.astype(BF16)))
            st_ref[h] = st * e_last[:, ks] + kv_ref[ic, h]
            ms = jnp.mean(o * o, axis=-1, keepdims=True)
            o_ref[rs, vs] = (o * lax.rsqrt(ms + EPS) * onw).astype(o_ref.dtype)


def _gla(q, k, v, g, out_norm_w, batch, seq):
    rows = min(GLA_ROWS, seq)
    nblk = seq // rows
    n_chunks = rows // GLA_CHUNK
    row = lambda w: pl.BlockSpec((rows, w), lambda b, i: (b * nblk + i, 0))
    return pl.pallas_call(
        functools.partial(_gla_kernel, rows=rows),
        grid=(batch, nblk),
        in_specs=[row(GLA_QK), row(GLA_QK), row(SEQ_WIDTH), row(GLA_QK), _const_spec((1, GLA_DV))],
        out_specs=row(SEQ_WIDTH),
        out_shape=jax.ShapeDtypeStruct((batch * seq, SEQ_WIDTH), BF16),
        scratch_shapes=[pltpu.VMEM((GLA_HEADS, GLA_DV, GLA_DK), F32),
                        pltpu.VMEM((rows, GLA_QK), BF16),
                        pltpu.VMEM((n_chunks, GLA_QK), F32),
                        pltpu.VMEM((n_chunks, GLA_HEADS, GLA_CHUNK, GLA_CHUNK), BF16),
                        pltpu.VMEM((n_chunks, GLA_HEADS, GLA_DV, GLA_DK), F32)],
        compiler_params=_params(2),
        name="gla_scan",
    )(q, k, v, g, out_norm_w.reshape(1, GLA_DV))


def _block_chunks(rows):
    return rows // S5_T


def _to_offset_major(x, scratch_ref):
    bc = _block_chunks(x.shape[0])
    cols = []
    for k in range(x.shape[1] // LANES):
        scratch_ref[k] = x[:, k * LANES:(k + 1) * LANES]
        cols.append(jnp.concatenate(
            [scratch_ref[k, pl.ds(s, bc, stride=S5_T), :] for s in range(S5_T)], axis=0))
    return jnp.concatenate(cols, axis=1)


def _add_from_offset_major(y, base_ref, scratch_ref, out_ref):
    bc = _block_chunks(y.shape[0])
    for k in range(y.shape[1] // LANES):
        ls = slice(k * LANES, (k + 1) * LANES)
        for s in range(S5_T):
            scratch_ref[k, pl.ds(s, bc, stride=S5_T), :] = y[s * bc:(s + 1) * bc, ls]
        out_ref[:, ls] = base_ref[:, ls] + scratch_ref[k]


def _s5_proj_kernel(x_ref, nw_ref, w_ref, u_ref, z_ref, xq_ref, perm_ref):
    h = _to_offset_major(_rms_norm_rows(x_ref[...], nw_ref[...]), perm_ref).astype(BF16)
    bc = _block_chunks(h.shape[0])
    for j in range(0, S5_TILES, 2):
        uu = jnp.dot(h, w_ref[:, j * LANES:(j + 2) * LANES], preferred_element_type=F32).astype(BF16)
        for s in range(S5_T):
            u_ref[j, s] = uu[s * bc:(s + 1) * bc, :LANES]
            u_ref[j + 1, s] = uu[s * bc:(s + 1) * bc, LANES:]
    _mm_store(h, w_ref, SEQ_WIDTH, SEQ_WIDTH + MIX_WIDTH, z_ref)
    _mm_store(h, w_ref, SEQ_WIDTH + MIX_WIDTH, SEQ_WIDTH + MIX_WIDTH + XA_WIDTH, xq_ref)


def _tile_offset_major_spec(rows):
    return pl.BlockSpec((S5_TILES, S5_T, _block_chunks(rows), LANES), lambda i: (0, 0, i, 0))


def _s5_proj(x2, norm_w, w_in):
    tok = x2.shape[0]
    tm = PROJ_ROWS
    n_in = w_in.shape[1]
    row = lambda w: pl.BlockSpec((tm, w), lambda i: (i, 0))
    return pl.pallas_call(
        _s5_proj_kernel,
        grid=(tok // tm,),
        in_specs=[row(D_MODEL), _const_spec((1, D_MODEL)), _const_spec((D_MODEL, n_in))],
        out_specs=[_tile_offset_major_spec(tm), row(MIX_WIDTH), row(XA_WIDTH)],
        out_shape=[jax.ShapeDtypeStruct((S5_TILES, S5_T, tok // S5_T, LANES), BF16),
                   jax.ShapeDtypeStruct((tok, MIX_WIDTH), BF16), jax.ShapeDtypeStruct((tok, XA_WIDTH), BF16)],
        scratch_shapes=[pltpu.VMEM((D_MODEL // LANES, tm, LANES), F32)],
        compiler_params=_params(1),
        name="s5_proj",
    )(x2, norm_w.reshape(1, D_MODEL), w_in.astype(BF16))


def _s5_operators(lam_re, lam_im, log_step, b_re, b_im, c_re, c_im):
    ng, t, p = S5_TILE_GROUPS, S5_T, S5_STATE
    grp = lambda tail: pl.BlockSpec((ng,) + tail, lambda j: (j,) + (0,) * len(tail))
    f32 = lambda tail: jax.ShapeDtypeStruct((S5_GROUPS,) + tail, F32)
    conv, tost_re, tost_im, from_re, from_im, pow_re, pow_im = pl.pallas_call(
        _s5_ops_kernel,
        grid=(S5_TILES,),
        in_specs=[grp((1, p)), grp((1, p)), grp((1, 1)), grp((S5_GROUP, p)), grp((S5_GROUP, p)),
                  grp((S5_GROUP, p)), grp((S5_GROUP, p))],
        out_specs=[grp((S5_K, S5_K)), grp((S5_K, p)), grp((S5_K, p)), grp((S5_K, p)), grp((S5_K, p)),
                   grp((SUBLANES, p)), grp((SUBLANES, p))],
        out_shape=[jax.ShapeDtypeStruct((S5_GROUPS, S5_K, S5_K), BF16), f32((S5_K, p)), f32((S5_K, p)),
                   f32((S5_K, p)), f32((S5_K, p)), f32((SUBLANES, p)), f32((SUBLANES, p))],
        compiler_params=_params(1),
        name="s5_ops",
    )(lam_re.reshape(S5_GROUPS, 1, p), lam_im.reshape(S5_GROUPS, 1, p), log_step.reshape(S5_GROUPS, 1, 1),
      jnp.transpose(b_re, (0, 2, 1)), jnp.transpose(b_im, (0, 2, 1)), c_re, c_im)
    to_st = jnp.transpose(jnp.concatenate([tost_re, tost_im], axis=-1), (0, 2, 1)).astype(BF16)
    from_st = jnp.concatenate([from_re, from_im], axis=-1).astype(BF16)
    tiles = lambda a: a.reshape(S5_TILES, ng, SUBLANES, p).transpose(0, 2, 1, 3).reshape(
        S5_TILES, SUBLANES, ng * p)
    tile4 = lambda a: a.reshape((S5_TILES, ng) + a.shape[1:])
    return tile4(conv), tile4(to_st), tile4(from_st), tiles(pow_re), tiles(pow_im)


def _s5_ops_kernel(lre_ref, lim_ref, ls_ref, btr_ref, bti_ref, cr_ref, ci_ref,
                   conv_ref, tsr_ref, tsi_ref, fr_ref, fi_ref, pwr_ref, pwi_ref):
    t, hh, p = S5_T, S5_GROUP, S5_STATE
    hp = lax.Precision.HIGHEST
    m_rows = lax.broadcasted_iota(jnp.int32, (3 * SUBLANES, p), 0).astype(F32)
    j_rows = (lax.broadcasted_iota(jnp.int32, (SUBLANES, p), 0) + 1).astype(F32) * t
    lane_s = lax.broadcasted_iota(jnp.int32, (hh, S5_K), 1) // hh
    for g in range(S5_TILE_GROUPS):
        lre, lim = lre_ref[g], lim_ref[g]
        dt = jnp.exp(ls_ref[g])
        mag = jnp.exp(lre * dt * m_rows)
        ang = lim * dt * m_rows
        pw_re, pw_im = mag * jnp.cos(ang), mag * jnp.sin(ang)
        magj = jnp.exp(lre * dt * j_rows)
        angj = lim * dt * j_rows
        pwr_ref[g] = magj * jnp.cos(angj)
        pwi_ref[g] = magj * jnp.sin(angj)
        den = lre * lre + lim * lim
        nr, ni = pw_re[1:2] - 1.0, pw_im[1:2]
        coef_re = (nr * lre + ni * lim) / den
        coef_im = (ni * lre - nr * lim) / den
        bt_re, bt_im = btr_ref[g], bti_ref[g]
        bb_re = coef_re * bt_re - coef_im * bt_im
        bb_im = coef_re * bt_im + coef_im * bt_re
        c_re, c_im = cr_ref[g], ci_ref[g]
        ca_re = [c_re * pw_re[m:m + 1] - c_im * pw_im[m:m + 1] for m in range(t + 1)]
        ca_im = [c_re * pw_im[m:m + 1] + c_im * pw_re[m:m + 1] for m in range(t + 1)]
        fr_ref[g] = jnp.concatenate(ca_re[1:], axis=0)
        fi_ref[g] = -jnp.concatenate(ca_im[1:], axis=0)
        kwide = (lax.dot_general(jnp.concatenate(ca_re[:t], axis=0), jnp.concatenate([bb_re] * t, axis=0),
                                 (((1,), (1,)), ((), ())), precision=hp, preferred_element_type=F32)
                 - lax.dot_general(jnp.concatenate(ca_im[:t], axis=0), jnp.concatenate([bb_im] * t, axis=0),
                                   (((1,), (1,)), ((), ())), precision=hp, preferred_element_type=F32))
        for ti in range(t):
            blk = jnp.zeros((hh, S5_K), F32)
            for m in range(ti + 1):
                blk = jnp.where(lane_s == ti - m, kwide[m * hh:(m + 1) * hh, :], blk)
            conv_ref[g, ti * hh:(ti + 1) * hh, :] = blk.astype(BF16)
        for s in range(t):
            wr, wi = pw_re[t - 1 - s:t - s], pw_im[t - 1 - s:t - s]
            tsr_ref[g, s * hh:(s + 1) * hh, :] = wr * bb_re - wi * bb_im
            tsi_ref[g, s * hh:(s + 1) * hh, :] = wr * bb_im + wi * bb_re


def _s5_core_kernel(u_ref, conv_ref, tost_ref, fromst_ref, pr_ref, pi_ref, y_ref,
                    cr_ref, ci_ref, zr_ref, zi_ref, sr_ref, si_ref, r_ref, *, n):
    @pl.when(pl.program_id(2) == 0)
    def _():
        cr_ref[...] = jnp.zeros_like(cr_ref)
        ci_ref[...] = jnp.zeros_like(ci_ref)

    half = S5_STATE
    ng = S5_TILE_GROUPS
    at = [u_ref[0, s].T for s in range(S5_T)]
    z = []
    for g in range(ng):
        rg = jnp.concatenate([a[g * S5_GROUP:(g + 1) * S5_GROUP, :] for a in at], axis=0)
        r_ref[g] = rg
        z.append(jnp.dot(tost_ref[0, g], rg, preferred_element_type=F32))
    for q in range(ng // 2):
        ls = slice(q * LANES, (q + 1) * LANES)
        zr_ref[:, ls] = jnp.concatenate([z[2 * q][:half], z[2 * q + 1][:half]], axis=0).T
        zi_ref[:, ls] = jnp.concatenate([z[2 * q][half:], z[2 * q + 1][half:]], axis=0).T

    pr = pr_ref[0]
    pi_ = pi_ref[0]
    width = ng * half
    rows = lax.broadcasted_iota(jnp.int32, (SUBLANES, width), 0)

    def cmul_add(xr, xi, ar, ai, br, bi):
        return xr + ar * br - ai * bi, xi + ar * bi + ai * br

    def tile_step(it, carry):
        c_r, c_i = carry
        r0 = pl.multiple_of(it * SUBLANES, SUBLANES)
        xr = zr_ref[pl.ds(r0, SUBLANES), :]
        xi = zi_ref[pl.ds(r0, SUBLANES), :]
        for k in (1, 2, 4):
            keep = rows >= k
            shr = jnp.where(keep, pltpu.roll(xr, k, 0), 0.0)
            shi = jnp.where(keep, pltpu.roll(xi, k, 0), 0.0)
            xr, xi = cmul_add(xr, xi, pr[k - 1:k, :], pi_[k - 1:k, :], shr, shi)
        s_r, s_i = cmul_add(xr, xi, pr, pi_, c_r, c_i)
        first = rows == 0
        sr_ref[pl.ds(r0, SUBLANES), :] = jnp.where(first, c_r, pltpu.roll(s_r, 1, 0))
        si_ref[pl.ds(r0, SUBLANES), :] = jnp.where(first, c_i, pltpu.roll(s_i, 1, 0))
        return (jnp.broadcast_to(s_r[SUBLANES - 1:, :], (SUBLANES, width)),
                jnp.broadcast_to(s_i[SUBLANES - 1:, :], (SUBLANES, width)))

    c_r, c_i = lax.fori_loop(0, n // SUBLANES, tile_step, (cr_ref[...], ci_ref[...]))
    cr_ref[...] = c_r
    ci_ref[...] = c_i

    yt = []
    for q in range(ng // 2):
        ls = slice(q * LANES, (q + 1) * LANES)
        srt = sr_ref[:, ls].T
        sit = si_ref[:, ls].T
        for e in range(2):
            g = 2 * q + e
            hs = slice(e * half, (e + 1) * half)
            sprev = jnp.concatenate([srt[hs], sit[hs]], axis=0).astype(BF16)
            yt.append(jnp.dot(conv_ref[0, g], r_ref[g], preferred_element_type=F32)
                      + jnp.dot(fromst_ref[0, g], sprev, preferred_element_type=F32))
    for t in range(S5_T):
        blk = jnp.concatenate([y[t * S5_GROUP:(t + 1) * S5_GROUP, :] for y in yt], axis=0)
        y_ref[0, t] = blk.astype(BF16).T


def _s5_core(u4, ops, batch, seq):
    conv, to_st, from_st, pow_re, pow_im = ops
    blk = min(S5_BLOCK, seq)
    nblk = seq // blk
    n = blk // S5_T
    width = S5_TILE_GROUPS * S5_STATE
    tok_spec = pl.BlockSpec((1, S5_T, n, LANES), lambda j, b, i: (j, 0, b * nblk + i, 0))
    tile_spec = lambda s: pl.BlockSpec((1,) + s, lambda j, b, i: (j,) + (0,) * len(s))
    return pl.pallas_call(
        functools.partial(_s5_core_kernel, n=n),
        grid=(S5_TILES, batch, nblk),
        in_specs=[tok_spec, tile_spec((S5_TILE_GROUPS, S5_K, S5_K)),
                  tile_spec((S5_TILE_GROUPS, 2 * S5_STATE, S5_K)),
                  tile_spec((S5_TILE_GROUPS, S5_K, 2 * S5_STATE)),
                  tile_spec((SUBLANES, width)), tile_spec((SUBLANES, width))],
        out_specs=tok_spec,
        out_shape=jax.ShapeDtypeStruct(u4.shape, BF16),
        scratch_shapes=[pltpu.VMEM((SUBLANES, width), F32), pltpu.VMEM((SUBLANES, width), F32),
                        pltpu.VMEM((n, width), F32), pltpu.VMEM((n, width), F32),
                        pltpu.VMEM((n, width), F32), pltpu.VMEM((n, width), F32),
                        pltpu.VMEM((S5_TILE_GROUPS, S5_K, n), BF16)],
        compiler_params=_params(3),
        name="s5_core",
    )(u4, conv, to_st, from_st, pow_re, pow_im)


def _gelu_tanh(x):
    return 0.5 * x * (1.0 + jnp.tanh(math.sqrt(2.0 / math.pi) * (x + 0.044715 * (x * x * x))))


def _s5_post_kernel(y_ref, u_ref, d_ref, wg_ref, bg_ref, o_ref, act_ref):
    bc = y_ref.shape[2]
    for j in range(S5_TILES):
        ls = slice(j * LANES, (j + 1) * LANES)
        for s in range(S5_T):
            act_ref[s * bc:(s + 1) * bc, ls] = _gelu_tanh(
                y_ref[j, s].astype(F32) + d_ref[:, ls] * u_ref[j, s].astype(F32))
    act = act_ref[...]
    gate = jax.nn.sigmoid(jnp.dot(act.astype(BF16), wg_ref[...], preferred_element_type=F32) + bg_ref[...])
    o_ref[...] = (act * gate).astype(o_ref.dtype)


def _s5_post(y4, u4, d, w_glu, b_glu):
    tok = y4.shape[2] * S5_T
    tm = PROJ_ROWS
    t4 = _tile_offset_major_spec(tm)
    return pl.pallas_call(
        _s5_post_kernel,
        grid=(tok // tm,),
        in_specs=[t4, t4, _const_spec((1, SEQ_WIDTH)), _const_spec((SEQ_WIDTH, SEQ_WIDTH)),
                  _const_spec((1, SEQ_WIDTH))],
        out_specs=pl.BlockSpec((tm, SEQ_WIDTH), lambda i: (i, 0)),
        out_shape=jax.ShapeDtypeStruct((tok, SEQ_WIDTH), BF16),
        scratch_shapes=[pltpu.VMEM((tm, SEQ_WIDTH), F32)],
        compiler_params=_params(1),
        name="s5_post",
    )(y4, u4, d.reshape(1, SEQ_WIDTH), w_glu.astype(BF16), b_glu.reshape(1, SEQ_WIDTH))


def _mix_kernel(seq_ref, z_ref, xq_ref, k_ref, v_ref, wo_ref, x_ref, fw_ref, o_ref, *perm_ref,
                final, offset_major):
    def silu(zc):
        zc = zc.astype(F32)
        return zc * jax.nn.sigmoid(zc)

    acc = None if offset_major else x_ref[...]
    for c0 in range(0, SEQ_WIDTH, MIX_CHUNK):
        cs = slice(c0, c0 + MIX_CHUNK)
        y_seq = (seq_ref[:, cs].astype(F32) * silu(z_ref[:, cs])).astype(BF16)
        part = jnp.dot(y_seq, wo_ref[cs, :], preferred_element_type=F32)
        acc = part if acc is None else acc + part
    for h in range(XA_HEADS):
        hs = slice(h * XA_HEAD_DIM, (h + 1) * XA_HEAD_DIM)
        zs = slice(SEQ_WIDTH + h * XA_HEAD_DIM, SEQ_WIDTH + (h + 1) * XA_HEAD_DIM)
        s = _dot_nt(xq_ref[:, hs], k_ref[0, :, hs]) * (XA_HEAD_DIM ** -0.5)
        s = s - jnp.max(s, axis=-1, keepdims=True)
        p = jnp.exp(s)
        p = (p / jnp.sum(p, axis=-1, keepdims=True)).astype(BF16)
        xa = jnp.dot(p, v_ref[0, :, hs], preferred_element_type=F32)
        y_xa = (xa * silu(z_ref[:, zs])).astype(BF16)
        acc = acc + jnp.dot(y_xa, wo_ref[zs, :], preferred_element_type=F32)
    if offset_major:
        _add_from_offset_major(acc, x_ref, perm_ref[0], o_ref)
        if final:
            o_ref[...] = _rms_norm_rows(o_ref[...], fw_ref[...])
    else:
        o_ref[...] = _rms_norm_rows(acc, fw_ref[...]) if final else acc


def _mix(seq_out, z, xq, kv, layer, w_out, x2, final_w, batch, seq, final, offset_major):
    tok = x2.shape[0]
    n_mem = kv.shape[1] // batch
    tm = PROJ_ROWS if offset_major else min(MIX_ROWS, seq)
    per_b = seq // tm
    row = lambda w: pl.BlockSpec((tm, w), lambda i: (i, 0))
    k_spec = pl.BlockSpec((1, n_mem, XA_WIDTH), lambda i: (layer, i // per_b, 0))
    v_spec = pl.BlockSpec((1, n_mem, XA_WIDTH), lambda i: (layer, i // per_b, 1))
    scratch = [pltpu.VMEM((D_MODEL // LANES, tm, LANES), F32)] if offset_major else []
    return pl.pallas_call(
        functools.partial(_mix_kernel, final=final, offset_major=offset_major),
        grid=(tok // tm,),
        in_specs=[row(SEQ_WIDTH), row(MIX_WIDTH), row(XA_WIDTH), k_spec, v_spec,
                  _const_spec((MIX_WIDTH, D_MODEL)), row(D_MODEL), _const_spec((1, D_MODEL))],
        out_specs=row(D_MODEL),
        out_shape=jax.ShapeDtypeStruct((tok, D_MODEL), F32),
        scratch_shapes=scratch,
        compiler_params=_params(1),
        name="mix_final" if final else "mix",
    )(seq_out, z, xq, kv, kv, w_out.astype(BF16), x2, final_w.reshape(1, D_MODEL))


def kernel(x, mem, norm_w, mem_norm_w, gla_w_in, gla_w_gate_up, gla_gate_bias, gla_out_norm_w, s5_w_in, s5_lam_re, s5_lam_im, s5_log_step, s5_b_re, s5_b_im, s5_c_re, s5_c_im, s5_d, s5_w_glu, s5_b_glu, xa_w_kv, w_out, final_norm_w):
    batch, seq, _ = x.shape
    depth = norm_w.shape[0]
    x2 = x.reshape(batch * seq, D_MODEL)
    kv = _mem_kv(mem.reshape(-1, D_MODEL), mem_norm_w, xa_w_kv.astype(BF16))
    for i in range(depth):
        j = i // 2
        if i % 2 == 0:
            q, k, v, g, z, xq = _gla_proj(x2, norm_w[i], gla_w_in[j], gla_w_gate_up[j], gla_gate_bias[j])
            seq_out = _gla(q, k, v, g, gla_out_norm_w[j], batch, seq)
        else:
            u4, z, xq = _s5_proj(x2, norm_w[i], s5_w_in[j])
            ops = _s5_operators(s5_lam_re[j], s5_lam_im[j], s5_log_step[j], s5_b_re[j], s5_b_im[j],
                                s5_c_re[j], s5_c_im[j])
            y4 = _s5_core(u4, ops, batch, seq)
            seq_out = _s5_post(y4, u4, s5_d[j], s5_w_glu[j], s5_b_glu[j])
        x2 = _mix(seq_out, z, xq, kv, i, w_out[i], x2, final_norm_w, batch, seq,
                  final=(i == depth - 1), offset_major=(i % 2 == 1))
    return x2.reshape(batch, seq, D_MODEL)
```

```python
import functools
import math

import jax
import jax.numpy as jnp
from jax import lax
from jax.experimental import pallas as pl
from jax.experimental.pallas import tpu as pltpu

F32 = jnp.float32
BF16 = jnp.bfloat16

D_MODEL = 1024
EPS = 1e-6
SEQ_WIDTH = 2 * D_MODEL
XA_HEADS = 4
XA_HEAD_DIM = D_MODEL // XA_HEADS
XA_WIDTH = D_MODEL
MIX_WIDTH = SEQ_WIDTH + XA_WIDTH
GLA_HEADS = 4
GLA_DK = 128
GLA_DV = 512
GLA_QK = GLA_HEADS * GLA_DK
GLA_RANK = 16
GLA_TAU = 16.0
GLA_CHUNK = 64
S5_GROUP = 16
S5_GROUPS = SEQ_WIDTH // S5_GROUP
S5_STATE = 64

LANES = 128
SUBLANES = 8
S5_T = 16
S5_TILE_GROUPS = LANES // S5_GROUP
S5_TILES = SEQ_WIDTH // LANES
S5_K = S5_T * S5_GROUP

PROJ_ROWS = 512
GLA_ROWS = 1024
S5_BLOCK = 4096
MIX_ROWS = 1024
VMEM_LIMIT = 56 * 1024 * 1024


def _const_spec(shape):
    nd = len(shape)
    return pl.BlockSpec(shape, lambda *_: (0,) * nd, pipeline_mode=pl.Buffered(1))


def _params(n_axes, flags=None):
    return pltpu.CompilerParams(dimension_semantics=("arbitrary",) * n_axes,
                                vmem_limit_bytes=VMEM_LIMIT, flags=flags)


def _rms_norm_rows(x, w):
    ms = jnp.mean(x * x, axis=-1, keepdims=True)
    return x * lax.rsqrt(ms + EPS) * w


def _silu(z):
    return z * jax.nn.sigmoid(z)


def _mm_store(h, w_ref, lo, hi, out_ref, chunk=512, act=None):
    for c in range(lo, hi, chunk):
        r = jnp.dot(h, w_ref[:, c:c + chunk], preferred_element_type=F32)
        out_ref[:, c - lo:c - lo + chunk] = (r if act is None else act(r)).astype(out_ref.dtype)


def _mem_kv_kernel(mem_ref, nw_ref, w_ref, kv_ref):
    mn = _rms_norm_rows(mem_ref[...], nw_ref[...]).astype(BF16)
    kv_ref[0] = jnp.dot(mn, w_ref[0], preferred_element_type=F32).astype(BF16)


def _mem_kv(mem2, mem_norm_w, w_kv_bf):
    depth = w_kv_bf.shape[0]
    rows, n_mem = mem2.shape[0], 256
    nb = rows // n_mem
    return pl.pallas_call(
        _mem_kv_kernel,
        grid=(depth, nb),
        in_specs=[pl.BlockSpec((n_mem, D_MODEL), lambda i, b: (b, 0)),
                  pl.BlockSpec((1, D_MODEL), lambda i, b: (0, 0)),
                  pl.BlockSpec((1, D_MODEL, 2 * XA_WIDTH), lambda i, b: (i, 0, 0))],
        out_specs=pl.BlockSpec((1, n_mem, 2 * XA_WIDTH), lambda i, b: (i, b, 0)),
        out_shape=jax.ShapeDtypeStruct((depth, rows, 2 * XA_WIDTH), BF16),
        compiler_params=_params(2),
        name="mem_kv",
    )(mem2, mem_norm_w.reshape(1, D_MODEL), w_kv_bf)


def _log_sigmoid(x):
    return jnp.minimum(x, 0.0) - jnp.log(1.0 + jnp.exp(-jnp.abs(x)))


def _gla_proj_kernel(x_ref, nw_ref, w_ref, wr_ref, wg_ref, gb_ref,
                     q_ref, k_ref, v_ref, g_ref, z_ref, xq_ref):
    h = _rms_norm_rows(x_ref[...], nw_ref[...]).astype(BF16)
    _mm_store(h, w_ref, 0, GLA_QK, q_ref)
    _mm_store(h, w_ref, GLA_QK, 2 * GLA_QK, k_ref)
    _mm_store(h, w_ref, 2 * GLA_QK, 2 * GLA_QK + SEQ_WIDTH, v_ref)
    z0 = 2 * GLA_QK + SEQ_WIDTH
    _mm_store(h, w_ref, z0, z0 + MIX_WIDTH, z_ref, act=_silu)
    _mm_store(h, w_ref, z0 + MIX_WIDTH, z0 + MIX_WIDTH + XA_WIDTH, xq_ref)
    r = jnp.dot(h, wr_ref[...], preferred_element_type=F32).astype(BF16)
    pre = jnp.dot(r, wg_ref[...], preferred_element_type=F32) + gb_ref[...]
    g_ref[...] = _log_sigmoid(pre) * (1.0 / GLA_TAU)


def _gla_proj(x2, norm_w, w_in, w_gate_up, gate_bias):
    tok = x2.shape[0]
    tm = PROJ_ROWS
    qk = GLA_QK
    r0 = 2 * qk + SEQ_WIDTH
    w_main = jnp.concatenate([w_in[:, :r0], w_in[:, r0 + GLA_RANK:]], axis=1).astype(BF16)
    w_r = jnp.pad(w_in[:, r0:r0 + GLA_RANK], ((0, 0), (0, LANES - GLA_RANK))).astype(BF16)
    w_g = jnp.pad(w_gate_up, ((0, LANES - GLA_RANK), (0, 0))).astype(BF16)
    n_main = w_main.shape[1]
    row = lambda w: pl.BlockSpec((tm, w), lambda i: (i, 0))
    return pl.pallas_call(
        _gla_proj_kernel,
        grid=(tok // tm,),
        in_specs=[row(D_MODEL), _const_spec((1, D_MODEL)), _const_spec((D_MODEL, n_main)),
                  _const_spec((D_MODEL, LANES)), _const_spec((LANES, qk)), _const_spec((1, qk))],
        out_specs=[row(qk), row(qk), row(SEQ_WIDTH), row(qk), row(MIX_WIDTH), row(XA_WIDTH)],
        out_shape=[jax.ShapeDtypeStruct((tok, qk), BF16), jax.ShapeDtypeStruct((tok, qk), BF16),
                   jax.ShapeDtypeStruct((tok, SEQ_WIDTH), BF16), jax.ShapeDtypeStruct((tok, qk), F32),
                   jax.ShapeDtypeStruct((tok, MIX_WIDTH), BF16), jax.ShapeDtypeStruct((tok, XA_WIDTH), BF16)],
        compiler_params=_params(1),
        name="gla_proj",
    )(x2, norm_w.reshape(1, D_MODEL), w_main, w_r, w_g, gate_bias.reshape(1, qk))


def _dot_nt(a, b):
    return lax.dot_general(a, b, (((1,), (1,)), ((), ())), preferred_element_type=F32)


def _dot_tn(a, b):
    return lax.dot_general(a, b, (((0,), (0,)), ((), ())), preferred_element_type=F32)


def _gla_kernel(q_ref, k_ref, v_ref, g_ref, onw_ref, o_ref, st_ref, qd_ref, el_ref, att_ref, kv_ref, *, rows):
    @pl.when(pl.program_id(1) == 0)
    def _():
        st_ref[...] = jnp.zeros_like(st_ref)

    c = GLA_CHUNK
    ri = lax.broadcasted_iota(jnp.int32, (c, c), 0)
    ci = lax.broadcasted_iota(jnp.int32, (c, c), 1)
    causal = ri >= ci
    tri = causal.astype(BF16)
    tri2 = jnp.concatenate([tri, tri], axis=1)
    scale = GLA_DK ** -0.5
    onw = onw_ref[...]
    n_chunks = rows // c

    for ic in range(n_chunks):
        rs = slice(ic * c, (ic + 1) * c)
        gc = g_ref[rs, :]
        g_hi = gc.astype(BF16)
        g_lo = (gc - g_hi.astype(F32)).astype(BF16)
        bcum = jnp.dot(tri2, jnp.concatenate([g_hi, g_lo], axis=0),
                       preferred_element_type=F32)
        b_last = bcum[c - 1:c, :]
        qc = q_ref[rs, :].astype(F32)
        kc = k_ref[rs, :].astype(F32)
        q_dec = (qc * (jnp.exp(bcum) * scale)).astype(BF16)
        k_inv = (kc * jnp.exp(-bcum)).astype(BF16)
        k_end = (kc * jnp.exp(b_last - bcum)).astype(BF16)
        qd_ref[rs, :] = q_dec
        el_ref[ic:ic + 1, :] = jnp.exp(b_last)
        for h in range(GLA_HEADS):
            ks = slice(h * GLA_DK, (h + 1) * GLA_DK)
            vs = slice(h * GLA_DV, (h + 1) * GLA_DV)
            att_ref[ic, h] = jnp.where(causal, _dot_nt(q_dec[:, ks], k_inv[:, ks]), 0.0).astype(BF16)
            kv_ref[ic, h] = _dot_tn(v_ref[rs, vs], k_end[:, ks])

    for ic in range(n_chunks):
        rs = slice(ic * c, (ic + 1) * c)
        e_last = el_ref[ic:ic + 1, :]
        for h in range(GLA_HEADS):
            ks = slice(h * GLA_DK, (h + 1) * GLA_DK)
            vs = slice(h * GLA_DV, (h + 1) * GLA_DV)
            st = st_ref[h]
            o = (jnp.dot(att_ref[ic, h], v_ref[rs, vs], preferred_element_type=F32)
                 + _dot_nt(qd_ref[rs, ks], st.astype(BF16)))
            st_ref[h] = st * e_last[:, ks] + kv_ref[ic, h]
            ms = jnp.mean(o * o, axis=-1, keepdims=True)
            o_ref[rs, vs] = (o * lax.rsqrt(ms + EPS) * onw).astype(o_ref.dtype)


def _gla(q, k, v, g, out_norm_w, batch, seq):
    rows = min(GLA_ROWS, seq)
    nblk = seq // rows
    n_chunks = rows // GLA_CHUNK
    row = lambda w: pl.BlockSpec((rows, w), lambda b, i: (b * nblk + i, 0))
    return pl.pallas_call(
        functools.partial(_gla_kernel, rows=rows),
        grid=(batch, nblk),
        in_specs=[row(GLA_QK), row(GLA_QK), row(SEQ_WIDTH), row(GLA_QK), _const_spec((1, GLA_DV))],
        out_specs=row(SEQ_WIDTH),
        out_shape=jax.ShapeDtypeStruct((batch * seq, SEQ_WIDTH), BF16),
        scratch_shapes=[pltpu.VMEM((GLA_HEADS, GLA_DV, GLA_DK), F32),
                        pltpu.VMEM((rows, GLA_QK), BF16),
                        pltpu.VMEM((n_chunks, GLA_QK), F32),
                        pltpu.VMEM((n_chunks, GLA_HEADS, GLA_CHUNK, GLA_CHUNK), BF16),
                        pltpu.VMEM((n_chunks, GLA_HEADS, GLA_DV, GLA_DK), F32)],
        compiler_params=_params(2),
        name="gla_scan",
    )(q, k, v, g, out_norm_w.reshape(1, GLA_DV))


def _block_chunks(rows):
    return rows // S5_T


def _to_offset_major(x, scratch_ref):
    bc = _block_chunks(x.shape[0])
    cols = []
    for k in range(x.shape[1] // LANES):
        scratch_ref[k] = x[:, k * LANES:(k + 1) * LANES]
        cols.append(jnp.concatenate(
            [scratch_ref[k, pl.ds(s, bc, stride=S5_T), :] for s in range(S5_T)], axis=0))
    return jnp.concatenate(cols, axis=1)


def _add_from_offset_major(y, base_ref, scratch_ref, out_ref):
    bc = _block_chunks(y.shape[0])
    for k in range(y.shape[1] // LANES):
        ls = slice(k * LANES, (k + 1) * LANES)
        for s in range(S5_T):
            scratch_ref[k, pl.ds(s, bc, stride=S5_T), :] = y[s * bc:(s + 1) * bc, ls]
        out_ref[:, ls] = base_ref[:, ls] + scratch_ref[k]


def _s5_proj_kernel(x_ref, nw_ref, w_ref, u_ref, z_ref, xq_ref, perm_ref):
    h = _to_offset_major(_rms_norm_rows(x_ref[...], nw_ref[...]), perm_ref).astype(BF16)
    bc = _block_chunks(h.shape[0])
    for j in range(0, S5_TILES, 2):
        uu = jnp.dot(h, w_ref[:, j * LANES:(j + 2) * LANES], preferred_element_type=F32).astype(BF16)
        for s in range(S5_T):
            u_ref[j, s] = uu[s * bc:(s + 1) * bc, :LANES]
            u_ref[j + 1, s] = uu[s * bc:(s + 1) * bc, LANES:]
    _mm_store(h, w_ref, SEQ_WIDTH, SEQ_WIDTH + MIX_WIDTH, z_ref)
    _mm_store(h, w_ref, SEQ_WIDTH + MIX_WIDTH, SEQ_WIDTH + MIX_WIDTH + XA_WIDTH, xq_ref)


def _tile_offset_major_spec(rows):
    return pl.BlockSpec((S5_TILES, S5_T, _block_chunks(rows), LANES), lambda i: (0, 0, i, 0))


def _s5_proj(x2, norm_w, w_in):
    tok = x2.shape[0]
    tm = PROJ_ROWS
    n_in = w_in.shape[1]
    row = lambda w: pl.BlockSpec((tm, w), lambda i: (i, 0))
    return pl.pallas_call(
        _s5_proj_kernel,
        grid=(tok // tm,),
        in_specs=[row(D_MODEL), _const_spec((1, D_MODEL)), _const_spec((D_MODEL, n_in))],
        out_specs=[_tile_offset_major_spec(tm), row(MIX_WIDTH), row(XA_WIDTH)],
        out_shape=[jax.ShapeDtypeStruct((S5_TILES, S5_T, tok // S5_T, LANES), BF16),
                   jax.ShapeDtypeStruct((tok, MIX_WIDTH), BF16), jax.ShapeDtypeStruct((tok, XA_WIDTH), BF16)],
        scratch_shapes=[pltpu.VMEM((D_MODEL // LANES, tm, LANES), F32)],
        compiler_params=_params(1),
        name="s5_proj",
    )(x2, norm_w.reshape(1, D_MODEL), w_in.astype(BF16))


def _s5_operators(lam_re, lam_im, log_step, b_re, b_im, c_re, c_im):
    ng, t, p = S5_TILE_GROUPS, S5_T, S5_STATE
    grp = lambda tail: pl.BlockSpec((ng,) + tail, lambda j: (j,) + (0,) * len(tail))
    f32 = lambda tail: jax.ShapeDtypeStruct((S5_GROUPS,) + tail, F32)
    conv, tost_re, tost_im, from_re, from_im, pow_re, pow_im = pl.pallas_call(
        _s5_ops_kernel,
        grid=(S5_TILES,),
        in_specs=[grp((1, p)), grp((1, p)), grp((1, 1)), grp((S5_GROUP, p)), grp((S5_GROUP, p)),
                  grp((S5_GROUP, p)), grp((S5_GROUP, p))],
        out_specs=[grp((S5_K, S5_K)), grp((S5_K, p)), grp((S5_K, p)), grp((S5_K, p)), grp((S5_K, p)),
                   grp((SUBLANES, p)), grp((SUBLANES, p))],
        out_shape=[jax.ShapeDtypeStruct((S5_GROUPS, S5_K, S5_K), BF16), f32((S5_K, p)), f32((S5_K, p)),
                   f32((S5_K, p)), f32((S5_K, p)), f32((SUBLANES, p)), f32((SUBLANES, p))],
        compiler_params=_params(1),
        name="s5_ops",
    )(lam_re.reshape(S5_GROUPS, 1, p), lam_im.reshape(S5_GROUPS, 1, p), log_step.reshape(S5_GROUPS, 1, 1),
      jnp.transpose(b_re, (0, 2, 1)), jnp.transpose(b_im, (0, 2, 1)), c_re, c_im)
    to_st = jnp.transpose(jnp.concatenate([tost_re, tost_im], axis=-1), (0, 2, 1)).astype(BF16)
    from_st = jnp.concatenate([from_re, from_im], axis=-1).astype(BF16)
    tiles = lambda a: a.reshape(S5_TILES, ng, SUBLANES, p).transpose(0, 2, 1, 3).reshape(
        S5_TILES, SUBLANES, ng * p)
    tile4 = lambda a: a.reshape((S5_TILES, ng) + a.shape[1:])
    return tile4(conv), tile4(to_st), tile4(from_st), tiles(pow_re), tiles(pow_im)


def _s5_ops_kernel(lre_ref, lim_ref, ls_ref, btr_ref, bti_ref, cr_ref, ci_ref,
                   conv_ref, tsr_ref, tsi_ref, fr_ref, fi_ref, pwr_ref, pwi_ref):
    t, hh, p = S5_T, S5_GROUP, S5_STATE
    hp = lax.Precision.HIGHEST
    m_rows = lax.broadcasted_iota(jnp.int32, (3 * SUBLANES, p), 0).astype(F32)
    j_rows = (lax.broadcasted_iota(jnp.int32, (SUBLANES, p), 0) + 1).astype(F32) * t
    lane_s = lax.broadcasted_iota(jnp.int32, (hh, S5_K), 1) // hh
    for g in range(S5_TILE_GROUPS):
        lre, lim = lre_ref[g], lim_ref[g]
        dt = jnp.exp(ls_ref[g])
        mag = jnp.exp(lre * dt * m_rows)
        ang = lim * dt * m_rows
        pw_re, pw_im = mag * jnp.cos(ang), mag * jnp.sin(ang)
        magj = jnp.exp(lre * dt * j_rows)
        angj = lim * dt * j_rows
        pwr_ref[g] = magj * jnp.cos(angj)
        pwi_ref[g] = magj * jnp.sin(angj)
        den = lre * lre + lim * lim
        nr, ni = pw_re[1:2] - 1.0, pw_im[1:2]
        coef_re = (nr * lre + ni * lim) / den
        coef_im = (ni * lre - nr * lim) / den
        bt_re, bt_im = btr_ref[g], bti_ref[g]
        bb_re = coef_re * bt_re - coef_im * bt_im
        bb_im = coef_re * bt_im + coef_im * bt_re
        c_re, c_im = cr_ref[g], ci_ref[g]
        ca_re = [c_re * pw_re[m:m + 1] - c_im * pw_im[m:m + 1] for m in range(t + 1)]
        ca_im = [c_re * pw_im[m:m + 1] + c_im * pw_re[m:m + 1] for m in range(t + 1)]
        fr_ref[g] = jnp.concatenate(ca_re[1:], axis=0)
        fi_ref[g] = -jnp.concatenate(ca_im[1:], axis=0)
        kwide = (lax.dot_general(jnp.concatenate(ca_re[:t], axis=0), jnp.concatenate([bb_re] * t, axis=0),
                                 (((1,), (1,)), ((), ())), precision=hp, preferred_element_type=F32)
                 - lax.dot_general(jnp.concatenate(ca_im[:t], axis=0), jnp.concatenate([bb_im] * t, axis=0),
                                   (((1,), (1,)), ((), ())), precision=hp, preferred_element_type=F32))
        for ti in range(t):
            blk = jnp.zeros((hh, S5_K), F32)
            for m in range(ti + 1):
                blk = jnp.where(lane_s == ti - m, kwide[m * hh:(m + 1) * hh, :], blk)
            conv_ref[g, ti * hh:(ti + 1) * hh, :] = blk.astype(BF16)
        for s in range(t):
            wr, wi = pw_re[t - 1 - s:t - s], pw_im[t - 1 - s:t - s]
            tsr_ref[g, s * hh:(s + 1) * hh, :] = wr * bb_re - wi * bb_im
            tsi_ref[g, s * hh:(s + 1) * hh, :] = wr * bb_im + wi * bb_re


def _s5_core_kernel(u_ref, conv_ref, tost_ref, fromst_ref, pr_ref, pi_ref, y_ref,
                    cr_ref, ci_ref, zr_ref, zi_ref, sr_ref, si_ref, r_ref, *, n):
    @pl.when(pl.program_id(2) == 0)
    def _():
        cr_ref[...] = jnp.zeros_like(cr_ref)
        ci_ref[...] = jnp.zeros_like(ci_ref)

    half = S5_STATE
    ng = S5_TILE_GROUPS
    at = [u_ref[0, s].T for s in range(S5_T)]
    z = []
    for g in range(ng):
        rg = jnp.concatenate([a[g * S5_GROUP:(g + 1) * S5_GROUP, :] for a in at], axis=0)
        r_ref[g] = rg
        z.append(jnp.dot(tost_ref[0, g], rg, preferred_element_type=F32))
    for q in range(ng // 2):
        ls = slice(q * LANES, (q + 1) * LANES)
        zr_ref[:, ls] = jnp.concatenate([z[2 * q][:half], z[2 * q + 1][:half]], axis=0).T
        zi_ref[:, ls] = jnp.concatenate([z[2 * q][half:], z[2 * q + 1][half:]], axis=0).T

    pr = pr_ref[0]
    pi_ = pi_ref[0]
    width = ng * half
    rows = lax.broadcasted_iota(jnp.int32, (SUBLANES, width), 0)

    def cmul_add(xr, xi, ar, ai, br, bi):
        return xr + ar * br - ai * bi, xi + ar * bi + ai * br

    def tile_step(it, carry):
        c_r, c_i = carry
        r0 = pl.multiple_of(it * SUBLANES, SUBLANES)
        xr = zr_ref[pl.ds(r0, SUBLANES), :]
        xi = zi_ref[pl.ds(r0, SUBLANES), :]
        for k in (1, 2, 4):
            keep = rows >= k
            shr = jnp.where(keep, pltpu.roll(xr, k, 0), 0.0)
            shi = jnp.where(keep, pltpu.roll(xi, k, 0), 0.0)
            xr, xi = cmul_add(xr, xi, pr[k - 1:k, :], pi_[k - 1:k, :], shr, shi)
        s_r, s_i = cmul_add(xr, xi, pr, pi_, c_r, c_i)
        first = rows == 0
        sr_ref[pl.ds(r0, SUBLANES), :] = jnp.where(first, c_r, pltpu.roll(s_r, 1, 0))
        si_ref[pl.ds(r0, SUBLANES), :] = jnp.where(first, c_i, pltpu.roll(s_i, 1, 0))
        return (jnp.broadcast_to(s_r[SUBLANES - 1:, :], (SUBLANES, width)),
                jnp.broadcast_to(s_i[SUBLANES - 1:, :], (SUBLANES, width)))

    c_r, c_i = lax.fori_loop(0, n // SUBLANES, tile_step, (cr_ref[...], ci_ref[...]))
    cr_ref[...] = c_r
    ci_ref[...] = c_i

    yt = []
    for q in range(ng // 2):
        ls = slice(q * LANES, (q + 1) * LANES)
        srt = sr_ref[:, ls].T
        sit = si_ref[:, ls].T
        for e in range(2):
            g = 2 * q + e
            hs = slice(e * half, (e + 1) * half)
            sprev = jnp.concatenate([srt[hs], sit[hs]], axis=0).astype(BF16)
            yt.append(jnp.dot(conv_ref[0, g], r_ref[g], preferred_element_type=F32)
                      + jnp.dot(fromst_ref[0, g], sprev, preferred_element_type=F32))
    for t in range(S5_T):
        blk = jnp.concatenate([y[t * S5_GROUP:(t + 1) * S5_GROUP, :] for y in yt], axis=0)
        y_ref[0, t] = blk.astype(BF16).T


def _s5_core(u4, ops, batch, seq):
    conv, to_st, from_st, pow_re, pow_im = ops
    blk = min(S5_BLOCK, seq)
    nblk = seq // blk
    n = blk // S5_T
    width = S5_TILE_GROUPS * S5_STATE
    tok_spec = pl.BlockSpec((1, S5_T, n, LANES), lambda j, b, i: (j, 0, b * nblk + i, 0))
    tile_spec = lambda s: pl.BlockSpec((1,) + s, lambda j, b, i: (j,) + (0,) * len(s))
    return pl.pallas_call(
        functools.partial(_s5_core_kernel, n=n),
        grid=(S5_TILES, batch, nblk),
        in_specs=[tok_spec, tile_spec((S5_TILE_GROUPS, S5_K, S5_K)),
                  tile_spec((S5_TILE_GROUPS, 2 * S5_STATE, S5_K)),
                  tile_spec((S5_TILE_GROUPS, S5_K, 2 * S5_STATE)),
                  tile_spec((SUBLANES, width)), tile_spec((SUBLANES, width))],
        out_specs=tok_spec,
        out_shape=jax.ShapeDtypeStruct(u4.shape, BF16),
        scratch_shapes=[pltpu.VMEM((SUBLANES, width), F32), pltpu.VMEM((SUBLANES, width), F32),
                        pltpu.VMEM((n, width), F32), pltpu.VMEM((n, width), F32),
                        pltpu.VMEM((n, width), F32), pltpu.VMEM((n, width), F32),
                        pltpu.VMEM((S5_TILE_GROUPS, S5_K, n), BF16)],
        compiler_params=_params(3),
        name="s5_core",
    )(u4, conv, to_st, from_st, pow_re, pow_im)


def _gelu_tanh(x):
    return 0.5 * x * (1.0 + jnp.tanh(math.sqrt(2.0 / math.pi) * (x + 0.044715 * (x * x * x))))


def _s5_post_kernel(y_ref, u_ref, d_ref, wg_ref, bg_ref, o_ref, act_ref):
    bc = y_ref.shape[2]
    for j in range(S5_TILES):
        ls = slice(j * LANES, (j + 1) * LANES)
        for s in range(S5_T):
            act_ref[s * bc:(s + 1) * bc, ls] = _gelu_tanh(
                y_ref[j, s].astype(F32) + d_ref[:, ls] * u_ref[j, s].astype(F32))
    act = act_ref[...]
    gate = jax.nn.sigmoid(jnp.dot(act.astype(BF16), wg_ref[...], preferred_element_type=F32) + bg_ref[...])
    o_ref[...] = (act * gate).astype(o_ref.dtype)


def _s5_post(y4, u4, d, w_glu, b_glu):
    tok = y4.shape[2] * S5_T
    tm = PROJ_ROWS
    t4 = _tile_offset_major_spec(tm)
    return pl.pallas_call(
        _s5_post_kernel,
        grid=(tok // tm,),
        in_specs=[t4, t4, _const_spec((1, SEQ_WIDTH)), _const_spec((SEQ_WIDTH, SEQ_WIDTH)),
                  _const_spec((1, SEQ_WIDTH))],
        out_specs=pl.BlockSpec((tm, SEQ_WIDTH), lambda i: (i, 0)),
        out_shape=jax.ShapeDtypeStruct((tok, SEQ_WIDTH), BF16),
        scratch_shapes=[pltpu.VMEM((tm, SEQ_WIDTH), F32)],
        compiler_params=_params(1),
        name="s5_post",
    )(y4, u4, d.reshape(1, SEQ_WIDTH), w_glu.astype(BF16), b_glu.reshape(1, SEQ_WIDTH))


def _mix_kernel(seq_ref, z_ref, xq_ref, k_ref, v_ref, wo_ref, x_ref, fw_ref, o_ref, *perm_ref,
                final, offset_major):
    if offset_major:
        gate = lambda cs: _silu(z_ref[:, cs].astype(F32))
        y_seq = (seq_ref[...].astype(F32) * gate(slice(0, SEQ_WIDTH))).astype(BF16)
    else:
        gate = lambda cs: z_ref[:, cs].astype(F32)
        y_seq = seq_ref[...] * z_ref[:, :SEQ_WIDTH]
    part = jnp.dot(y_seq, wo_ref[:SEQ_WIDTH, :], preferred_element_type=F32)
    acc = part if offset_major else x_ref[...] + part
    for h in range(XA_HEADS):
        hs = slice(h * XA_HEAD_DIM, (h + 1) * XA_HEAD_DIM)
        zs = slice(SEQ_WIDTH + h * XA_HEAD_DIM, SEQ_WIDTH + (h + 1) * XA_HEAD_DIM)
        s = _dot_nt(xq_ref[:, hs], k_ref[0, :, hs]) * (XA_HEAD_DIM ** -0.5)
        s = s - jnp.max(s, axis=-1, keepdims=True)
        p = jnp.exp(s)
        p = (p / jnp.sum(p, axis=-1, keepdims=True)).astype(BF16)
        xa = jnp.dot(p, v_ref[0, :, hs], preferred_element_type=F32)
        y_xa = (xa * gate(zs)).astype(BF16)
        acc = acc + jnp.dot(y_xa, wo_ref[zs, :], preferred_element_type=F32)
    if offset_major:
        _add_from_offset_major(acc, x_ref, perm_ref[0], o_ref)
        if final:
            o_ref[...] = _rms_norm_rows(o_ref[...], fw_ref[...])
    else:
        o_ref[...] = _rms_norm_rows(acc, fw_ref[...]) if final else acc


def _mix(seq_out, z, xq, kv, layer, w_out, x2, final_w, batch, seq, final, offset_major):
    tok = x2.shape[0]
    n_mem = kv.shape[1] // batch
    tm = PROJ_ROWS if offset_major else min(MIX_ROWS, seq)
    per_b = seq // tm
    row = lambda w: pl.BlockSpec((tm, w), lambda i: (i, 0))
    k_spec = pl.BlockSpec((1, n_mem, XA_WIDTH), lambda i: (layer, i // per_b, 0))
    v_spec = pl.BlockSpec((1, n_mem, XA_WIDTH), lambda i: (layer, i // per_b, 1))
    scratch = [pltpu.VMEM((D_MODEL // LANES, tm, LANES), F32)] if offset_major else []
    return pl.pallas_call(
        functools.partial(_mix_kernel, final=final, offset_major=offset_major),
        grid=(tok // tm,),
        in_specs=[row(SEQ_WIDTH), row(MIX_WIDTH), row(XA_WIDTH), k_spec, v_spec,
                  _const_spec((MIX_WIDTH, D_MODEL)), row(D_MODEL), _const_spec((1, D_MODEL))],
        out_specs=row(D_MODEL),
        out_shape=jax.ShapeDtypeStruct((tok, D_MODEL), F32),
        scratch_shapes=scratch,
        compiler_params=_params(1),
        name="mix_final" if final else "mix",
    )(seq_out, z, xq, kv, kv, w_out.astype(BF16), x2, final_w.reshape(1, D_MODEL))


def kernel(x, mem, norm_w, mem_norm_w, gla_w_in, gla_w_gate_up, gla_gate_bias, gla_out_norm_w, s5_w_in, s5_lam_re, s5_lam_im, s5_log_step, s5_b_re, s5_b_im, s5_c_re, s5_c_im, s5_d, s5_w_glu, s5_b_glu, xa_w_kv, w_out, final_norm_w):
    batch, seq, _ = x.shape
    depth = norm_w.shape[0]
    x2 = x.reshape(batch * seq, D_MODEL)
    kv = _mem_kv(mem.reshape(-1, D_MODEL), mem_norm_w, xa_w_kv.astype(BF16))
    for i in range(depth):
        j = i // 2
        if i % 2 == 0:
            q, k, v, g, z, xq = _gla_proj(x2, norm_w[i], gla_w_in[j], gla_w_gate_up[j], gla_gate_bias[j])
            seq_out = _gla(q, k, v, g, gla_out_norm_w[j], batch, seq)
        else:
            u4, z, xq = _s5_proj(x2, norm_w[i], s5_w_in[j])
            ops = _s5_operators(s5_lam_re[j], s5_lam_im[j], s5_log_step[j], s5_b_re[j], s5_b_im[j],
                                s5_c_re[j], s5_c_im[j])
            y4 = _s5_core(u4, ops, batch, seq)
            seq_out = _s5_post(y4, u4, s5_d[j], s5_w_glu[j], s5_b_glu[j])
        x2 = _mix(seq_out, z, xq, kv, i, w_out[i], x2, final_norm_w, batch, seq,
                  final=(i == depth - 1), offset_major=(i % 2 == 1))
    return x2.reshape(batch, seq, D_MODEL)
```

```python
import functools
import math

import jax
import jax.numpy as jnp
from jax import lax
from jax.experimental import pallas as pl
from jax.experimental.pallas import tpu as pltpu

F32 = jnp.float32
BF16 = jnp.bfloat16

D_MODEL = 1024
EPS = 1e-6
SEQ_WIDTH = 2 * D_MODEL
XA_HEADS = 4
XA_HEAD_DIM = D_MODEL // XA_HEADS
XA_WIDTH = D_MODEL
MIX_WIDTH = SEQ_WIDTH + XA_WIDTH
GLA_HEADS = 4
GLA_DK = 128
GLA_DV = 512
GLA_QK = GLA_HEADS * GLA_DK
GLA_RANK = 16
GLA_TAU = 16.0
GLA_CHUNK = 64
S5_GROUP = 16
S5_GROUPS = SEQ_WIDTH // S5_GROUP
S5_STATE = 64

LANES = 128
SUBLANES = 8
S5_T = 16
S5_TILE_GROUPS = LANES // S5_GROUP
S5_TILES = SEQ_WIDTH // LANES
S5_K = S5_T * S5_GROUP

PROJ_ROWS = 512
GLA_ROWS = 1024
S5_BLOCK = 4096
MIX_CHUNK = 512
MIX_ROWS = 1024
VMEM_LIMIT = 56 * 1024 * 1024
MIX_VMEM_LIMIT = 60 * 1024 * 1024


def _const_spec(shape):
    nd = len(shape)
    return pl.BlockSpec(shape, lambda *_: (0,) * nd, pipeline_mode=pl.Buffered(1))


def _params(n_axes, vmem_limit=VMEM_LIMIT):
    return pltpu.CompilerParams(dimension_semantics=("arbitrary",) * n_axes,
                                vmem_limit_bytes=vmem_limit)


def _rms_norm_rows(x, w):
    ms = jnp.mean(x * x, axis=-1, keepdims=True)
    return x * lax.rsqrt(ms + EPS) * w


def _silu(z):
    return z * jax.nn.sigmoid(z)


def _mm_store(h, w_ref, lo, hi, out_ref, chunk=512, act=None):
    for c in range(lo, hi, chunk):
        r = jnp.dot(h, w_ref[:, c:c + chunk], preferred_element_type=F32)
        out_ref[:, c - lo:c - lo + chunk] = (r if act is None else act(r)).astype(out_ref.dtype)


def _mem_kv_kernel(mem_ref, nw_ref, w_ref, kv_ref):
    mn = _rms_norm_rows(mem_ref[...], nw_ref[...]).astype(BF16)
    kv_ref[0] = jnp.dot(mn, w_ref[0], preferred_element_type=F32).astype(BF16)


def _mem_kv(mem2, mem_norm_w, w_kv_bf):
    depth = w_kv_bf.shape[0]
    rows, n_mem = mem2.shape[0], 256
    nb = rows // n_mem
    return pl.pallas_call(
        _mem_kv_kernel,
        grid=(depth, nb),
        in_specs=[pl.BlockSpec((n_mem, D_MODEL), lambda i, b: (b, 0)),
                  pl.BlockSpec((1, D_MODEL), lambda i, b: (0, 0)),
                  pl.BlockSpec((1, D_MODEL, 2 * XA_WIDTH), lambda i, b: (i, 0, 0))],
        out_specs=pl.BlockSpec((1, n_mem, 2 * XA_WIDTH), lambda i, b: (i, b, 0)),
        out_shape=jax.ShapeDtypeStruct((depth, rows, 2 * XA_WIDTH), BF16),
        compiler_params=_params(2),
        name="mem_kv",
    )(mem2, mem_norm_w.reshape(1, D_MODEL), w_kv_bf)


def _log_sigmoid(x):
    return jnp.minimum(x, 0.0) - jnp.log(1.0 + jnp.exp(-jnp.abs(x)))


def _gla_proj_kernel(x_ref, nw_ref, w_ref, wr_ref, wg_ref, gb_ref,
                     q_ref, k_ref, v_ref, g_ref, z_ref, xq_ref):
    h = _rms_norm_rows(x_ref[...], nw_ref[...]).astype(BF16)
    _mm_store(h, w_ref, 0, GLA_QK, q_ref)
    _mm_store(h, w_ref, GLA_QK, 2 * GLA_QK, k_ref)
    _mm_store(h, w_ref, 2 * GLA_QK, 2 * GLA_QK + SEQ_WIDTH, v_ref)
    z0 = 2 * GLA_QK + SEQ_WIDTH
    _mm_store(h, w_ref, z0, z0 + MIX_WIDTH, z_ref, act=_silu)
    _mm_store(h, w_ref, z0 + MIX_WIDTH, z0 + MIX_WIDTH + XA_WIDTH, xq_ref)
    r = jnp.dot(h, wr_ref[...], preferred_element_type=F32).astype(BF16)
    pre = jnp.dot(r, wg_ref[...], preferred_element_type=F32) + gb_ref[...]
    g_ref[...] = _log_sigmoid(pre) * (1.0 / GLA_TAU)


def _gla_proj(x2, norm_w, w_in, w_gate_up, gate_bias):
    tok = x2.shape[0]
    tm = PROJ_ROWS
    qk = GLA_QK
    r0 = 2 * qk + SEQ_WIDTH
    w_main = jnp.concatenate([w_in[:, :r0], w_in[:, r0 + GLA_RANK:]], axis=1).astype(BF16)
    w_r = jnp.pad(w_in[:, r0:r0 + GLA_RANK], ((0, 0), (0, LANES - GLA_RANK))).astype(BF16)
    w_g = jnp.pad(w_gate_up, ((0, LANES - GLA_RANK), (0, 0))).astype(BF16)
    n_main = w_main.shape[1]
    row = lambda w: pl.BlockSpec((tm, w), lambda i: (i, 0))
    return pl.pallas_call(
        _gla_proj_kernel,
        grid=(tok // tm,),
        in_specs=[row(D_MODEL), _const_spec((1, D_MODEL)), _const_spec((D_MODEL, n_main)),
                  _const_spec((D_MODEL, LANES)), _const_spec((LANES, qk)), _const_spec((1, qk))],
        out_specs=[row(qk), row(qk), row(SEQ_WIDTH), row(qk), row(MIX_WIDTH), row(XA_WIDTH)],
        out_shape=[jax.ShapeDtypeStruct((tok, qk), BF16), jax.ShapeDtypeStruct((tok, qk), BF16),
                   jax.ShapeDtypeStruct((tok, SEQ_WIDTH), BF16), jax.ShapeDtypeStruct((tok, qk), F32),
                   jax.ShapeDtypeStruct((tok, MIX_WIDTH), BF16), jax.ShapeDtypeStruct((tok, XA_WIDTH), BF16)],
        compiler_params=_params(1),
        name="gla_proj",
    )(x2, norm_w.reshape(1, D_MODEL), w_main, w_r, w_g, gate_bias.reshape(1, qk))


def _dot_nt(a, b):
    return lax.dot_general(a, b, (((1,), (1,)), ((), ())), preferred_element_type=F32)


def _dot_tn(a, b):
    return lax.dot_general(a, b, (((0,), (0,)), ((), ())), preferred_element_type=F32)


def _gla_kernel(q_ref, k_ref, v_ref, g_ref, onw_ref, o_ref, st_ref, qd_ref, ke_ref, el_ref, att_ref, *, rows):
    @pl.when(pl.program_id(1) == 0)
    def _():
        st_ref[...] = jnp.zeros_like(st_ref)

    c = GLA_CHUNK
    ri = lax.broadcasted_iota(jnp.int32, (c, c), 0)
    ci = lax.broadcasted_iota(jnp.int32, (c, c), 1)
    causal = ri >= ci
    tri = causal.astype(BF16)
    tri2 = jnp.concatenate([tri, tri], axis=1)
    scale = GLA_DK ** -0.5
    onw = onw_ref[...]
    n_chunks = rows // c

    for ic in range(n_chunks):
        rs = slice(ic * c, (ic + 1) * c)
        gc = g_ref[rs, :]
        g_hi = gc.astype(BF16)
        g_lo = (gc - g_hi.astype(F32)).astype(BF16)
        bcum = jnp.dot(tri2, jnp.concatenate([g_hi, g_lo], axis=0),
                       preferred_element_type=F32)
        b_last = bcum[c - 1:c, :]
        qc = q_ref[rs, :].astype(F32)
        kc = k_ref[rs, :].astype(F32)
        q_dec = (qc * (jnp.exp(bcum) * scale)).astype(BF16)
        k_inv = (kc * jnp.exp(-bcum)).astype(BF16)
        k_end = (kc * jnp.exp(b_last - bcum)).astype(BF16)
        qd_ref[rs, :] = q_dec
        ke_ref[rs, :] = k_end
        el_ref[ic:ic + 1, :] = jnp.exp(b_last)
        for h in range(GLA_HEADS):
            ks = slice(h * GLA_DK, (h + 1) * GLA_DK)
            att_ref[ic, h] = jnp.where(causal, _dot_nt(q_dec[:, ks], k_inv[:, ks]), 0.0).astype(BF16)

    for ic in range(n_chunks):
        rs = slice(ic * c, (ic + 1) * c)
        e_last = el_ref[ic:ic + 1, :]
        for h in range(GLA_HEADS):
            ks = slice(h * GLA_DK, (h + 1) * GLA_DK)
            vs = slice(h * GLA_DV, (h + 1) * GLA_DV)
            st = st_ref[h]
            o = (jnp.dot(att_ref[ic, h], v_ref[rs, vs], preferred_element_type=F32)
                 + _dot_nt(qd_ref[rs, ks], st.astype(BF16)))
            st_ref[h] = st * e_last[:, ks] + _dot_tn(v_ref[rs, vs], ke_ref[rs, ks])
            ms = jnp.mean(o * o, axis=-1, keepdims=True)
            o_ref[rs, vs] = (o * lax.rsqrt(ms + EPS) * onw).astype(o_ref.dtype)


def _gla(q, k, v, g, out_norm_w, batch, seq):
    rows = min(GLA_ROWS, seq)
    nblk = seq // rows
    n_chunks = rows // GLA_CHUNK
    row = lambda w: pl.BlockSpec((rows, w), lambda b, i: (b * nblk + i, 0))
    return pl.pallas_call(
        functools.partial(_gla_kernel, rows=rows),
        grid=(batch, nblk),
        in_specs=[row(GLA_QK), row(GLA_QK), row(SEQ_WIDTH), row(GLA_QK), _const_spec((1, GLA_DV))],
        out_specs=row(SEQ_WIDTH),
        out_shape=jax.ShapeDtypeStruct((batch * seq, SEQ_WIDTH), BF16),
        scratch_shapes=[pltpu.VMEM((GLA_HEADS, GLA_DV, GLA_DK), F32),
                        pltpu.VMEM((rows, GLA_QK), BF16),
                        pltpu.VMEM((rows, GLA_QK), BF16),
                        pltpu.VMEM((n_chunks, GLA_QK), F32),
                        pltpu.VMEM((n_chunks, GLA_HEADS, GLA_CHUNK, GLA_CHUNK), BF16)],
        compiler_params=_params(2),
        name="gla_scan",
    )(q, k, v, g, out_norm_w.reshape(1, GLA_DV))


def _block_chunks(rows):
    return rows // S5_T


def _to_offset_major(x, scratch_ref):
    bc = _block_chunks(x.shape[0])
    cols = []
    for k in range(x.shape[1] // LANES):
        scratch_ref[k] = x[:, k * LANES:(k + 1) * LANES]
        cols.append(jnp.concatenate(
            [scratch_ref[k, pl.ds(s, bc, stride=S5_T), :] for s in range(S5_T)], axis=0))
    return jnp.concatenate(cols, axis=1)


def _add_from_offset_major(y, base_ref, scratch_ref, out_ref):
    blk = scratch_ref.shape[1]
    bc = _block_chunks(blk)
    for r0 in range(0, y.shape[0], blk):
        for k in range(y.shape[1] // LANES):
            ls = slice(k * LANES, (k + 1) * LANES)
            for s in range(S5_T):
                scratch_ref[k, pl.ds(s, bc, stride=S5_T), :] = y[r0 + s * bc:r0 + (s + 1) * bc, ls]
            out_ref[r0:r0 + blk, ls] = base_ref[r0:r0 + blk, ls] + scratch_ref[k]


def _s5_proj_kernel(x_ref, nw_ref, w_ref, u_ref, z_ref, xq_ref, perm_ref):
    h = _to_offset_major(_rms_norm_rows(x_ref[...], nw_ref[...]), perm_ref).astype(BF16)
    bc = _block_chunks(h.shape[0])
    for j in range(0, S5_TILES, 2):
        uu = jnp.dot(h, w_ref[:, j * LANES:(j + 2) * LANES], preferred_element_type=F32).astype(BF16)
        for s in range(S5_T):
            u_ref[j, s] = uu[s * bc:(s + 1) * bc, :LANES]
            u_ref[j + 1, s] = uu[s * bc:(s + 1) * bc, LANES:]
    _mm_store(h, w_ref, SEQ_WIDTH, SEQ_WIDTH + MIX_WIDTH, z_ref)
    _mm_store(h, w_ref, SEQ_WIDTH + MIX_WIDTH, SEQ_WIDTH + MIX_WIDTH + XA_WIDTH, xq_ref)


def _tile_offset_major_spec(rows):
    return pl.BlockSpec((S5_TILES, S5_T, _block_chunks(rows), LANES), lambda i: (0, 0, i, 0))


def _s5_proj(x2, norm_w, w_in):
    tok = x2.shape[0]
    tm = PROJ_ROWS
    n_in = w_in.shape[1]
    row = lambda w: pl.BlockSpec((tm, w), lambda i: (i, 0))
    return pl.pallas_call(
        _s5_proj_kernel,
        grid=(tok // tm,),
        in_specs=[row(D_MODEL), _const_spec((1, D_MODEL)), _const_spec((D_MODEL, n_in))],
        out_specs=[_tile_offset_major_spec(tm), row(MIX_WIDTH), row(XA_WIDTH)],
        out_shape=[jax.ShapeDtypeStruct((S5_TILES, S5_T, tok // S5_T, LANES), BF16),
                   jax.ShapeDtypeStruct((tok, MIX_WIDTH), BF16), jax.ShapeDtypeStruct((tok, XA_WIDTH), BF16)],
        scratch_shapes=[pltpu.VMEM((D_MODEL // LANES, tm, LANES), F32)],
        compiler_params=_params(1),
        name="s5_proj",
    )(x2, norm_w.reshape(1, D_MODEL), w_in.astype(BF16))


def _s5_operators(lam_re, lam_im, log_step, b_re, b_im, c_re, c_im):
    ng, t, p = S5_TILE_GROUPS, S5_T, S5_STATE
    grp = lambda tail: pl.BlockSpec((ng,) + tail, lambda j: (j,) + (0,) * len(tail))
    f32 = lambda tail: jax.ShapeDtypeStruct((S5_GROUPS,) + tail, F32)
    conv, tost_re, tost_im, from_re, from_im, pow_re, pow_im = pl.pallas_call(
        _s5_ops_kernel,
        grid=(S5_TILES,),
        in_specs=[grp((1, p)), grp((1, p)), grp((1, 1)), grp((S5_GROUP, p)), grp((S5_GROUP, p)),
                  grp((S5_GROUP, p)), grp((S5_GROUP, p))],
        out_specs=[grp((S5_K, S5_K)), grp((S5_K, p)), grp((S5_K, p)), grp((S5_K, p)), grp((S5_K, p)),
                   grp((SUBLANES, p)), grp((SUBLANES, p))],
        out_shape=[jax.ShapeDtypeStruct((S5_GROUPS, S5_K, S5_K), BF16), f32((S5_K, p)), f32((S5_K, p)),
                   f32((S5_K, p)), f32((S5_K, p)), f32((SUBLANES, p)), f32((SUBLANES, p))],
        compiler_params=_params(1),
        name="s5_ops",
    )(lam_re.reshape(S5_GROUPS, 1, p), lam_im.reshape(S5_GROUPS, 1, p), log_step.reshape(S5_GROUPS, 1, 1),
      jnp.transpose(b_re, (0, 2, 1)), jnp.transpose(b_im, (0, 2, 1)), c_re, c_im)
    to_st = jnp.transpose(jnp.concatenate([tost_re, tost_im], axis=-1), (0, 2, 1)).astype(BF16)
    from_st = jnp.concatenate([from_re, from_im], axis=-1).astype(BF16)
    tiles = lambda a: a.reshape(S5_TILES, ng, SUBLANES, p).transpose(0, 2, 1, 3).reshape(
        S5_TILES, SUBLANES, ng * p)
    tile4 = lambda a: a.reshape((S5_TILES, ng) + a.shape[1:])
    return tile4(conv), tile4(to_st), tile4(from_st), tiles(pow_re), tiles(pow_im)


def _s5_ops_kernel(lre_ref, lim_ref, ls_ref, btr_ref, bti_ref, cr_ref, ci_ref,
                   conv_ref, tsr_ref, tsi_ref, fr_ref, fi_ref, pwr_ref, pwi_ref):
    t, hh, p = S5_T, S5_GROUP, S5_STATE
    hp = lax.Precision.HIGHEST
    m_rows = lax.broadcasted_iota(jnp.int32, (3 * SUBLANES, p), 0).astype(F32)
    j_rows = (lax.broadcasted_iota(jnp.int32, (SUBLANES, p), 0) + 1).astype(F32) * t
    lane_s = lax.broadcasted_iota(jnp.int32, (hh, S5_K), 1) // hh
    for g in range(S5_TILE_GROUPS):
        lre, lim = lre_ref[g], lim_ref[g]
        dt = jnp.exp(ls_ref[g])
        mag = jnp.exp(lre * dt * m_rows)
        ang = lim * dt * m_rows
        pw_re, pw_im = mag * jnp.cos(ang), mag * jnp.sin(ang)
        magj = jnp.exp(lre * dt * j_rows)
        angj = lim * dt * j_rows
        pwr_ref[g] = magj * jnp.cos(angj)
        pwi_ref[g] = magj * jnp.sin(angj)
        den = lre * lre + lim * lim
        nr, ni = pw_re[1:2] - 1.0, pw_im[1:2]
        coef_re = (nr * lre + ni * lim) / den
        coef_im = (ni * lre - nr * lim) / den
        bt_re, bt_im = btr_ref[g], bti_ref[g]
        bb_re = coef_re * bt_re - coef_im * bt_im
        bb_im = coef_re * bt_im + coef_im * bt_re
        c_re, c_im = cr_ref[g], ci_ref[g]
        ca_re = [c_re * pw_re[m:m + 1] - c_im * pw_im[m:m + 1] for m in range(t + 1)]
        ca_im = [c_re * pw_im[m:m + 1] + c_im * pw_re[m:m + 1] for m in range(t + 1)]
        fr_ref[g] = jnp.concatenate(ca_re[1:], axis=0)
        fi_ref[g] = -jnp.concatenate(ca_im[1:], axis=0)
        kwide = (lax.dot_general(jnp.concatenate(ca_re[:t], axis=0), jnp.concatenate([bb_re] * t, axis=0),
                                 (((1,), (1,)), ((), ())), precision=hp, preferred_element_type=F32)
                 - lax.dot_general(jnp.concatenate(ca_im[:t], axis=0), jnp.concatenate([bb_im] * t, axis=0),
                                   (((1,), (1,)), ((), ())), precision=hp, preferred_element_type=F32))
        for ti in range(t):
            blk = jnp.zeros((hh, S5_K), F32)
            for m in range(ti + 1):
                blk = jnp.where(lane_s == ti - m, kwide[m * hh:(m + 1) * hh, :], blk)
            conv_ref[g, ti * hh:(ti + 1) * hh, :] = blk.astype(BF16)
        for s in range(t):
            wr, wi = pw_re[t - 1 - s:t - s], pw_im[t - 1 - s:t - s]
            tsr_ref[g, s * hh:(s + 1) * hh, :] = wr * bb_re - wi * bb_im
            tsi_ref[g, s * hh:(s + 1) * hh, :] = wr * bb_im + wi * bb_re


def _s5_core_kernel(u_ref, conv_ref, tost_ref, fromst_ref, pr_ref, pi_ref, y_ref,
                    cr_ref, ci_ref, zr_ref, zi_ref, sr_ref, si_ref, r_ref, *, n):
    @pl.when(pl.program_id(2) == 0)
    def _():
        cr_ref[...] = jnp.zeros_like(cr_ref)
        ci_ref[...] = jnp.zeros_like(ci_ref)

    half = S5_STATE
    ng = S5_TILE_GROUPS
    at = [u_ref[0, s].T for s in range(S5_T)]
    z = []
    for g in range(ng):
        rg = jnp.concatenate([a[g * S5_GROUP:(g + 1) * S5_GROUP, :] for a in at], axis=0)
        r_ref[g] = rg
        z.append(jnp.dot(tost_ref[0, g], rg, preferred_element_type=F32))
    for q in range(ng // 2):
        ls = slice(q * LANES, (q + 1) * LANES)
        zr_ref[:, ls] = jnp.concatenate([z[2 * q][:half], z[2 * q + 1][:half]], axis=0).T
        zi_ref[:, ls] = jnp.concatenate([z[2 * q][half:], z[2 * q + 1][half:]], axis=0).T

    pr = pr_ref[0]
    pi_ = pi_ref[0]
    width = ng * half
    rows = lax.broadcasted_iota(jnp.int32, (SUBLANES, width), 0)

    def cmul_add(xr, xi, ar, ai, br, bi):
        return xr + ar * br - ai * bi, xi + ar * bi + ai * br

    def tile_step(it, carry):
        c_r, c_i = carry
        r0 = pl.multiple_of(it * SUBLANES, SUBLANES)
        xr = zr_ref[pl.ds(r0, SUBLANES), :]
        xi = zi_ref[pl.ds(r0, SUBLANES), :]
        for k in (1, 2, 4):
            keep = rows >= k
            shr = jnp.where(keep, pltpu.roll(xr, k, 0), 0.0)
            shi = jnp.where(keep, pltpu.roll(xi, k, 0), 0.0)
            xr, xi = cmul_add(xr, xi, pr[k - 1:k, :], pi_[k - 1:k, :], shr, shi)
        s_r, s_i = cmul_add(xr, xi, pr, pi_, c_r, c_i)
        first = rows == 0
        sr_ref[pl.ds(r0, SUBLANES), :] = jnp.where(first, c_r, pltpu.roll(s_r, 1, 0))
        si_ref[pl.ds(r0, SUBLANES), :] = jnp.where(first, c_i, pltpu.roll(s_i, 1, 0))
        return (jnp.broadcast_to(s_r[SUBLANES - 1:, :], (SUBLANES, width)),
                jnp.broadcast_to(s_i[SUBLANES - 1:, :], (SUBLANES, width)))

    c_r, c_i = lax.fori_loop(0, n // SUBLANES, tile_step, (cr_ref[...], ci_ref[...]))
    cr_ref[...] = c_r
    ci_ref[...] = c_i

    yt = []
    for q in range(ng // 2):
        ls = slice(q * LANES, (q + 1) * LANES)
        srt = sr_ref[:, ls].T
        sit = si_ref[:, ls].T
        for e in range(2):
            g = 2 * q + e
            hs = slice(e * half, (e + 1) * half)
            sprev = jnp.concatenate([srt[hs], sit[hs]], axis=0).astype(BF16)
            yt.append(jnp.dot(conv_ref[0, g], r_ref[g], preferred_element_type=F32)
                      + jnp.dot(fromst_ref[0, g], sprev, preferred_element_type=F32))
    for t in range(S5_T):
        blk = jnp.concatenate([y[t * S5_GROUP:(t + 1) * S5_GROUP, :] for y in yt], axis=0)
        y_ref[0, t] = blk.astype(BF16).T


def _s5_core(u4, ops, batch, seq):
    conv, to_st, from_st, pow_re, pow_im = ops
    blk = min(S5_BLOCK, seq)
    nblk = seq // blk
    n = blk // S5_T
    width = S5_TILE_GROUPS * S5_STATE
    tok_spec = pl.BlockSpec((1, S5_T, n, LANES), lambda j, b, i: (j, 0, b * nblk + i, 0))
    tile_spec = lambda s: pl.BlockSpec((1,) + s, lambda j, b, i: (j,) + (0,) * len(s))
    return pl.pallas_call(
        functools.partial(_s5_core_kernel, n=n),
        grid=(S5_TILES, batch, nblk),
        in_specs=[tok_spec, tile_spec((S5_TILE_GROUPS, S5_K, S5_K)),
                  tile_spec((S5_TILE_GROUPS, 2 * S5_STATE, S5_K)),
                  tile_spec((S5_TILE_GROUPS, S5_K, 2 * S5_STATE)),
                  tile_spec((SUBLANES, width)), tile_spec((SUBLANES, width))],
        out_specs=tok_spec,
        out_shape=jax.ShapeDtypeStruct(u4.shape, BF16),
        scratch_shapes=[pltpu.VMEM((SUBLANES, width), F32), pltpu.VMEM((SUBLANES, width), F32),
                        pltpu.VMEM((n, width), F32), pltpu.VMEM((n, width), F32),
                        pltpu.VMEM((n, width), F32), pltpu.VMEM((n, width), F32),
                        pltpu.VMEM((S5_TILE_GROUPS, S5_K, n), BF16)],
        compiler_params=_params(3),
        name="s5_core",
    )(u4, conv, to_st, from_st, pow_re, pow_im)


def _gelu_tanh(x):
    a = -2.0 * math.sqrt(2.0 / math.pi) * math.log2(math.e)
    return x / (1.0 + jnp.exp2(x * (a + (a * 0.044715) * (x * x))))


def _s5_post_kernel(y_ref, u_ref, d_ref, wg_ref, bg_ref, o_ref, act_ref):
    bc = y_ref.shape[2]
    for j in range(S5_TILES):
        ls = slice(j * LANES, (j + 1) * LANES)
        for s in range(S5_T):
            act_ref[s * bc:(s + 1) * bc, ls] = _gelu_tanh(
                y_ref[j, s].astype(F32) + d_ref[:, ls] * u_ref[j, s].astype(F32))
    act = act_ref[...]
    gate = jax.nn.sigmoid(jnp.dot(act.astype(BF16), wg_ref[...], preferred_element_type=F32) + bg_ref[...])
    o_ref[...] = (act * gate).astype(o_ref.dtype)


def _s5_post(y4, u4, d, w_glu, b_glu):
    tok = y4.shape[2] * S5_T
    tm = PROJ_ROWS
    t4 = _tile_offset_major_spec(tm)
    return pl.pallas_call(
        _s5_post_kernel,
        grid=(tok // tm,),
        in_specs=[t4, t4, _const_spec((1, SEQ_WIDTH)), _const_spec((SEQ_WIDTH, SEQ_WIDTH)),
                  _const_spec((1, SEQ_WIDTH))],
        out_specs=pl.BlockSpec((tm, SEQ_WIDTH), lambda i: (i, 0)),
        out_shape=jax.ShapeDtypeStruct((tok, SEQ_WIDTH), BF16),
        scratch_shapes=[pltpu.VMEM((tm, SEQ_WIDTH), F32)],
        compiler_params=_params(1),
        name="s5_post",
    )(y4, u4, d.reshape(1, SEQ_WIDTH), w_glu.astype(BF16), b_glu.reshape(1, SEQ_WIDTH))


def _mix_kernel(seq_ref, z_ref, xq_ref, k_ref, v_ref, wo_ref, x_ref, fw_ref, o_ref, *perm_ref,
                final, offset_major):
    if offset_major:
        gate = lambda cs: _silu(z_ref[:, cs].astype(F32))
        acc = None
        for c0 in range(0, SEQ_WIDTH, MIX_CHUNK):
            cs = slice(c0, c0 + MIX_CHUNK)
            y_seq = (seq_ref[:, cs].astype(F32) * gate(cs)).astype(BF16)
            part = jnp.dot(y_seq, wo_ref[cs, :], preferred_element_type=F32)
            acc = part if acc is None else acc + part
    else:
        gate = lambda cs: z_ref[:, cs].astype(F32)
        y_seq = seq_ref[...] * z_ref[:, :SEQ_WIDTH]
        acc = x_ref[...] + jnp.dot(y_seq, wo_ref[:SEQ_WIDTH, :], preferred_element_type=F32)
    for h in range(XA_HEADS):
        hs = slice(h * XA_HEAD_DIM, (h + 1) * XA_HEAD_DIM)
        zs = slice(SEQ_WIDTH + h * XA_HEAD_DIM, SEQ_WIDTH + (h + 1) * XA_HEAD_DIM)
        s = _dot_nt(xq_ref[:, hs], k_ref[0, :, hs]) * (XA_HEAD_DIM ** -0.5)
        s = s - jnp.max(s, axis=-1, keepdims=True)
        p = jnp.exp(s)
        p = (p / jnp.sum(p, axis=-1, keepdims=True)).astype(BF16)
        xa = jnp.dot(p, v_ref[0, :, hs], preferred_element_type=F32)
        y_xa = (xa * gate(zs)).astype(BF16)
        acc = acc + jnp.dot(y_xa, wo_ref[zs, :], preferred_element_type=F32)
    if offset_major:
        _add_from_offset_major(acc, x_ref, perm_ref[0], o_ref)
        if final:
            o_ref[...] = _rms_norm_rows(o_ref[...], fw_ref[...])
    else:
        o_ref[...] = _rms_norm_rows(acc, fw_ref[...]) if final else acc


def _mix(seq_out, z, xq, kv, layer, w_out, x2, final_w, batch, seq, final, offset_major):
    tok = x2.shape[0]
    n_mem = kv.shape[1] // batch
    tm = min(MIX_ROWS, seq)
    per_b = seq // tm
    row = lambda w: pl.BlockSpec((tm, w), lambda i: (i, 0))
    k_spec = pl.BlockSpec((1, n_mem, XA_WIDTH), lambda i: (layer, i // per_b, 0))
    v_spec = pl.BlockSpec((1, n_mem, XA_WIDTH), lambda i: (layer, i // per_b, 1))
    scratch = [pltpu.VMEM((D_MODEL // LANES, PROJ_ROWS, LANES), F32)] if offset_major else []
    return pl.pallas_call(
        functools.partial(_mix_kernel, final=final, offset_major=offset_major),
        grid=(tok // tm,),
        in_specs=[row(SEQ_WIDTH), row(MIX_WIDTH), row(XA_WIDTH), k_spec, v_spec,
                  _const_spec((MIX_WIDTH, D_MODEL)), row(D_MODEL), _const_spec((1, D_MODEL))],
        out_specs=row(D_MODEL),
        out_shape=jax.ShapeDtypeStruct((tok, D_MODEL), F32),
        scratch_shapes=scratch,
        compiler_params=_params(1, vmem_limit=MIX_VMEM_LIMIT),
        name="mix_final" if final else "mix",
    )(seq_out, z, xq, kv, kv, w_out.astype(BF16), x2, final_w.reshape(1, D_MODEL))


def kernel(x, mem, norm_w, mem_norm_w, gla_w_in, gla_w_gate_up, gla_gate_bias, gla_out_norm_w, s5_w_in, s5_lam_re, s5_lam_im, s5_log_step, s5_b_re, s5_b_im, s5_c_re, s5_c_im, s5_d, s5_w_glu, s5_b_glu, xa_w_kv, w_out, final_norm_w):
    batch, seq, _ = x.shape
    depth = norm_w.shape[0]
    x2 = x.reshape(batch * seq, D_MODEL)
    kv = _mem_kv(mem.reshape(-1, D_MODEL), mem_norm_w, xa_w_kv.astype(BF16))
    for i in range(depth):
        j = i // 2
        if i % 2 == 0:
            q, k, v, g, z, xq = _gla_proj(x2, norm_w[i], gla_w_in[j], gla_w_gate_up[j], gla_gate_bias[j])
            seq_out = _gla(q, k, v, g, gla_out_norm_w[j], batch, seq)
        else:
            u4, z, xq = _s5_proj(x2, norm_w[i], s5_w_in[j])
            ops = _s5_operators(s5_lam_re[j], s5_lam_im[j], s5_log_step[j], s5_b_re[j], s5_b_im[j],
                                s5_c_re[j], s5_c_im[j])
            y4 = _s5_core(u4, ops, batch, seq)
            seq_out = _s5_post(y4, u4, s5_d[j], s5_w_glu[j], s5_b_glu[j])
        x2 = _mix(seq_out, z, xq, kv, i, w_out[i], x2, final_norm_w, batch, seq,
                  final=(i == depth - 1), offset_major=(i % 2 == 1))
    return x2.reshape(batch, seq, D_MODEL)
```
